```python
import jax, jax.numpy as jnp
from jax import lax
import numpy as np

D_MODEL = 1024
BATCH = 16
SEQ = 2048
DEPTH = 2

CHUNK = 64
EPS = 1e-6

POOL_WINDOWS = (2, 4, 8, 16)
POOL_WIDTH = D_MODEL
POOL_GROUP = POOL_WIDTH // len(POOL_WINDOWS)

LRU_WIDTH = D_MODEL
LRU_BLOCKS = 16
LRU_BLOCK = LRU_WIDTH // LRU_BLOCKS
CONV_WIDTH = 4
LRU_C = 8.0

SB_HEADS = 8
SB_HEAD_DIM = 128
SB_WIDTH = SB_HEADS * SB_HEAD_DIM
Q_BLOCK = 128

N_BRANCH = 3
IN_COLS = POOL_WIDTH + LRU_WIDTH + 3 * SB_WIDTH + N_BRANCH * D_MODEL
D_FF = ((8 * D_MODEL + 3 * 256 - 1) // (3 * 256)) * 256

kernel_name = "hybrid_pool_rglru_stickbreak_block"


def rms_norm(x, g):
    x32 = x.astype(jnp.float32)
    y = x32 * lax.rsqrt(jnp.mean(x32 * x32, axis=-1, keepdims=True) + EPS)
    return (y * g.astype(jnp.float32)).astype(x.dtype)


def pool_mixer(u, w_pool, pool_scale):
    S = u.shape[1]
    u32 = u.astype(jnp.float32)
    pos = jnp.arange(1, S + 1, dtype=jnp.float32)[None, :, None]
    outs = []
    for g, w in enumerate(POOL_WINDOWS):
        ug = u32[..., g * POOL_GROUP:(g + 1) * POOL_GROUP]
        cs = jnp.cumsum(ug, axis=1)
        cs_lag = jnp.pad(cs, ((0, 0), (w, 0), (0, 0)))[:, :S]
        mean = (cs - cs_lag) / jnp.minimum(pos, float(w))
        outs.append(mean - ug)
    p = jnp.stack(outs, axis=2).astype(u.dtype)
    y = jnp.einsum('bsgi,gij->bsgj', p, w_pool)
    return y.reshape(u.shape) * pool_scale


def rglru_mixer(u, conv_w, conv_b, w_rg, b_rg, w_ig, b_ig, lru_lambda):
    B, S, _ = u.shape
    up = jnp.pad(u, ((0, 0), (CONV_WIDTH - 1, 0), (0, 0)))
    xc = conv_b + conv_w[0] * up[:, 0:S]
    for k in range(1, CONV_WIDTH):
        xc = xc + conv_w[k] * up[:, k:k + S]
    xb = xc.reshape(B, S, LRU_BLOCKS, LRU_BLOCK)
    r = jax.nn.sigmoid(jnp.einsum('bshi,hij->bshj', xb, w_rg).reshape(B, S, LRU_WIDTH) + b_rg)
    i = jax.nn.sigmoid(jnp.einsum('bshi,hij->bshj', xb, w_ig).reshape(B, S, LRU_WIDTH) + b_ig)
    log_a = (-LRU_C * r.astype(jnp.float32)) * jax.nn.softplus(-lru_lambda.astype(jnp.float32))
    a = jnp.exp(log_a)
    mult = jnp.sqrt(-jnp.expm1(2.0 * log_a))
    b = mult * (i * xc).astype(jnp.float32)

    def combine(c1, c2):
        a1, b1 = c1
        a2, b2 = c2
        return a1 * a2, a2 * b1 + b2

    _, h = lax.associative_scan(combine, (a, b), axis=1)
    return h.astype(u.dtype)


def stick_breaking_attention(q, k, v):
    B, S, H, Dh = q.shape
    scale = Dh ** -0.5
    outs = []
    for start in range(0, S, Q_BLOCK):
        end = start + Q_BLOCK
        qb = q[:, start:end]
        kb = k[:, :end]
        vb = v[:, :end]
        z = jnp.einsum('bqhd,bkhd->bhqk', qb, kb).astype(jnp.float32) * scale
        qpos = jnp.arange(start, end)[:, None]
        kpos = jnp.arange(end)[None, :]
        causal = kpos < qpos
        log_beta = jax.nn.log_sigmoid(z)
        log_keep = jnp.where(causal, jax.nn.log_sigmoid(-z), 0.0)
        after = lax.cumsum(log_keep, axis=3, reverse=True) - log_keep
        w = jnp.where(causal, jnp.exp(log_beta + after), 0.0)
        outs.append(jnp.einsum('bhqk,bkhd->bqhd', w.astype(v.dtype), vb))
    return jnp.concatenate(outs, axis=1)


def setup_inputs(seed: int = 0) -> dict:
    key = jax.random.key(seed)
    ks = jax.random.split(key, 24)
    f32 = jnp.float32
    nrm = lambda k, shape, s: jax.random.normal(k, shape, f32) * s
    u = jax.random.uniform(ks[13], (DEPTH, LRU_WIDTH), f32, minval=0.9, maxval=0.999)
    p = u ** (1.0 / LRU_C)
    lru_lambda = jnp.log(p) - jnp.log1p(-p)
    return {
        "x": nrm(ks[0], (BATCH, SEQ, D_MODEL), 1.0),
        "norm_mix": 1.0 + nrm(ks[1], (DEPTH, D_MODEL), 0.02),
        "w_in": nrm(ks[2], (DEPTH, D_MODEL, IN_COLS), D_MODEL ** -0.5),
        "b_gate": nrm(ks[3], (DEPTH, N_BRANCH * D_MODEL), 0.01),
        "w_pool": nrm(ks[4], (DEPTH, len(POOL_WINDOWS), POOL_GROUP, POOL_GROUP), POOL_GROUP ** -0.5),
        "pool_scale": 1.0 + nrm(ks[5], (DEPTH, POOL_WIDTH), 0.1),
        "conv_w": nrm(ks[6], (DEPTH, CONV_WIDTH, LRU_WIDTH), CONV_WIDTH ** -0.5),
        "conv_b": nrm(ks[7], (DEPTH, LRU_WIDTH), 0.01),
        "w_rg": nrm(ks[8], (DEPTH, LRU_BLOCKS, LRU_BLOCK, LRU_BLOCK), LRU_BLOCK ** -0.5),
        "b_rg": nrm(ks[9], (DEPTH, LRU_WIDTH), 0.01),
        "w_ig": nrm(ks[10], (DEPTH, LRU_BLOCKS, LRU_BLOCK, LRU_BLOCK), LRU_BLOCK ** -0.5),
        "b_ig": nrm(ks[11], (DEPTH, LRU_WIDTH), 0.01),
        "lru_lambda": lru_lambda,
        "q_norm": 1.0 + nrm(ks[14], (DEPTH, SB_HEAD_DIM), 0.02),
        "k_norm": 1.0 + nrm(ks[15], (DEPTH, SB_HEAD_DIM), 0.02),
        "w_branch": nrm(ks[16], (DEPTH, N_BRANCH, D_MODEL, D_MODEL), D_MODEL ** -0.5),
        "w_out": nrm(ks[17], (DEPTH, D_MODEL, D_MODEL), D_MODEL ** -0.5),
        "norm_ffn": 1.0 + nrm(ks[18], (DEPTH, D_MODEL), 0.02),
        "w_ffn_gate": nrm(ks[19], (DEPTH, D_MODEL, D_FF), D_MODEL ** -0.5),
        "w_ffn_up": nrm(ks[20], (DEPTH, D_MODEL, D_FF), D_MODEL ** -0.5),
        "w_ffn_down": nrm(ks[21], (DEPTH, D_FF, D_MODEL), D_FF ** -0.5),
    }


def reference(x, norm_mix, w_in, b_gate, w_pool, pool_scale, conv_w, conv_b, w_rg, b_rg,
              w_ig, b_ig, lru_lambda, q_norm, k_norm, w_branch, w_out, norm_ffn,
              w_ffn_gate, w_ffn_up, w_ffn_down):
    B, S, _ = x.shape
    splits = np.cumsum([POOL_WIDTH, LRU_WIDTH, SB_WIDTH, SB_WIDTH, SB_WIDTH]).tolist()
    for l in range(DEPTH):
        h = rms_norm(x, norm_mix[l])
        proj = jnp.einsum('bsd,dc->bsc', h, w_in[l])
        u_pool, u_lru, q, k, v, g_logits = jnp.split(proj, splits, axis=-1)
        gates = jax.nn.sigmoid(g_logits + b_gate[l]).reshape(B, S, N_BRANCH, D_MODEL)

        y_pool = pool_mixer(u_pool, w_pool[l], pool_scale[l])
        y_lru = rglru_mixer(u_lru, conv_w[l], conv_b[l], w_rg[l], b_rg[l],
                            w_ig[l], b_ig[l], lru_lambda[l])
        qh = rms_norm(q.reshape(B, S, SB_HEADS, SB_HEAD_DIM), q_norm[l])
        kh = rms_norm(k.reshape(B, S, SB_HEADS, SB_HEAD_DIM), k_norm[l])
        vh = v.reshape(B, S, SB_HEADS, SB_HEAD_DIM)
        y_sb = stick_breaking_attention(qh, kh, vh).reshape(B, S, SB_WIDTH)

        merged = (gates[:, :, 0] * jnp.einsum('bsc,cd->bsd', y_pool, w_branch[l, 0])
                  + gates[:, :, 1] * jnp.einsum('bsc,cd->bsd', y_lru, w_branch[l, 1])
                  + gates[:, :, 2] * jnp.einsum('bsc,cd->bsd', y_sb, w_branch[l, 2]))
        x = x + jnp.einsum('bsd,de->bse', merged, w_out[l])

        h2 = rms_norm(x, norm_ffn[l])
        act = jax.nn.silu(jnp.einsum('bsd,df->bsf', h2, w_ffn_gate[l])) * jnp.einsum('bsd,df->bsf', h2, w_ffn_up[l])
        x = x + jnp.einsum('bsf,fd->bsd', act, w_ffn_down[l])
    return x
```

```python
import functools

import jax
import jax.numpy as jnp
from jax import lax
from jax.experimental import pallas as pl
from jax.experimental.pallas import tpu as pltpu

F32 = jnp.float32
BF16 = jnp.bfloat16

D_MODEL = 1024
EPS = 1e-6
N_POOL_GROUPS = 4
POOL_GROUP = 256
LRU_BLOCKS = 16
LRU_BLOCK = 64
CONV_WIDTH = 4
LRU_C = 8.0
SB_HEADS = 8
SB_HEAD_DIM = 128
D_FF = 2816
IN_COLS = 8 * D_MODEL

CH_TILE = 256
LRU_COL0 = D_MODEL // CH_TILE
Q_COL0 = 2 * D_MODEL // SB_HEAD_DIM
K_COL0 = 3 * D_MODEL // SB_HEAD_DIM
V_COL0 = 4 * D_MODEL // SB_HEAD_DIM
GATE_COL0 = 5

KEY_BLOCK = 128
ROW_TILE = 8
VMEM_LIMIT = 56 * 1024 * 1024


def _cparams(*sem):
    return pltpu.CompilerParams(dimension_semantics=sem, vmem_limit_bytes=VMEM_LIMIT)


def _const_spec(shape):
    zeros = (0,) * len(shape)
    return pl.BlockSpec(shape, lambda *_: zeros, pipeline_mode=pl.Buffered(1))


def _rms(x, g):
    return x * lax.rsqrt(jnp.mean(x * x, axis=-1, keepdims=True) + EPS) * g


def _inproj_kernel(x_ref, g_ref, w_ref, o_ref, *, col_chunk):
    h = _rms(x_ref[...], g_ref[...]).astype(BF16)
    for c in range(IN_COLS // col_chunk):
        cols = slice(c * col_chunk, (c + 1) * col_chunk)
        o_ref[:, cols] = jnp.dot(h, w_ref[:, cols], preferred_element_type=F32).astype(BF16)


def _inproj(x2, g, w, tm):
    T = x2.shape[0]
    return pl.pallas_call(
        functools.partial(_inproj_kernel, col_chunk=2048),
        grid=(T // tm,),
        in_specs=[pl.BlockSpec((tm, D_MODEL), lambda i: (i, 0)),
                  _const_spec((1, D_MODEL)),
                  _const_spec((D_MODEL, IN_COLS))],
        out_specs=pl.BlockSpec((tm, IN_COLS), lambda i: (i, 0)),
        out_shape=jax.ShapeDtypeStruct((T, IN_COLS), BF16),
        compiler_params=_cparams("parallel"),
        name="inproj",
    )(x2, g, w)


def _shift_rows(x, d, row, min_row):
    return jnp.where(row >= min_row, pltpu.roll(x, d, axis=0), 0.0)


def _mixers_kernel(up_ref, ul_ref, wpool_ref, pscale_ref, convw_ref, convb_ref, wgate_ref,
                   brg_ref, big_ref, lam_ref, yp_ref, yl_ref, a_scr, b_scr):
    S = up_ref.shape[0]
    j = pl.program_id(1)
    row = lax.broadcasted_iota(jnp.int32, (S, CH_TILE), 0)

    u = up_ref[...].astype(F32)
    s = u
    for k, d in enumerate((1, 2, 4, 8)):
        min_row = jnp.where(j >= k, d, S)
        s = s + _shift_rows(s, d, row, min_row)
    window = jnp.left_shift(2, j)
    count = jnp.minimum(row + 1, window).astype(F32)
    p = (s / count - u).astype(BF16)
    y = jnp.dot(p, wpool_ref[0], preferred_element_type=F32) * pscale_ref[...]
    yp_ref[...] = y.astype(BF16)

    ul = ul_ref[...].astype(F32)
    xc = convb_ref[...] + convw_ref[CONV_WIDTH - 1:CONV_WIDTH, :] * ul
    for k in range(CONV_WIDTH - 1):
        d = CONV_WIDTH - 1 - k
        xc = xc + convw_ref[k:k + 1, :] * _shift_rows(ul, d, row, d)
    gates = jnp.dot(xc.astype(BF16), wgate_ref[0], preferred_element_type=F32)
    r = jax.nn.sigmoid(gates[:, :CH_TILE] + brg_ref[...])
    ig = jax.nn.sigmoid(gates[:, CH_TILE:] + big_ref[...])
    lam = lam_ref[...]
    softplus_neg_lam = jnp.maximum(-lam, 0.0) + jnp.log1p(jnp.exp(-jnp.abs(lam)))
    log_a = (-LRU_C * r) * softplus_neg_lam
    a = jnp.exp(log_a)
    a_scr[...] = a
    b_scr[...] = jnp.sqrt(-jnp.tanh(log_a) * (a * a + 1.0)) * (ig * xc)

    sub = lax.broadcasted_iota(jnp.int32, (ROW_TILE, CH_TILE), 0)

    def scan_tile(t, h_prev):
        rows = pl.ds(pl.multiple_of(t * ROW_TILE, ROW_TILE), ROW_TILE)
        a = a_scr[rows, :]
        b = b_scr[rows, :]
        for d in (1, 2, 4):
            keep = sub >= d
            b = a * jnp.where(keep, pltpu.roll(b, d, axis=0), 0.0) + b
            a = a * jnp.where(keep, pltpu.roll(a, d, axis=0), 1.0)
        h = a * h_prev + b
        b_scr[rows, :] = h
        return jnp.broadcast_to(h[ROW_TILE - 1:ROW_TILE, :], (ROW_TILE, CH_TILE))

    lax.fori_loop(0, S // ROW_TILE, scan_tile, jnp.zeros((ROW_TILE, CH_TILE), F32), unroll=4)
    yl_ref[...] = b_scr[...].astype(BF16)


def _mixers(proj, B, S, wpool, pscale, convw, convb, wgate, brg, big, lam):
    T = B * S
    nt = D_MODEL // CH_TILE
    vec = pl.BlockSpec((1, CH_TILE), lambda b, j: (0, j))
    return pl.pallas_call(
        _mixers_kernel,
        grid=(B, nt),
        in_specs=[pl.BlockSpec((S, CH_TILE), lambda b, j: (b, j)),
                  pl.BlockSpec((S, CH_TILE), lambda b, j: (b, LRU_COL0 + j)),
                  pl.BlockSpec((1, POOL_GROUP, POOL_GROUP), lambda b, j: (j, 0, 0)),
                  vec,
                  pl.BlockSpec((CONV_WIDTH, CH_TILE), lambda b, j: (0, j)),
                  vec,
                  pl.BlockSpec((1, CH_TILE, 2 * CH_TILE), lambda b, j: (j, 0, 0)),
                  vec, vec, vec],
        out_specs=[pl.BlockSpec((S, CH_TILE), lambda b, j: (b, j)),
                   pl.BlockSpec((S, CH_TILE), lambda b, j: (b, j))],
        out_shape=[jax.ShapeDtypeStruct((T, D_MODEL), BF16),
                   jax.ShapeDtypeStruct((T, D_MODEL), BF16)],
        scratch_shapes=[pltpu.VMEM((S, CH_TILE), F32), pltpu.VMEM((S, CH_TILE), F32)],
        compiler_params=_cparams("parallel", "parallel"),
        name="mixers",
    )(proj, proj, wpool, pscale, convw, convb, wgate, brg, big, lam)


def _attn_kernel(q_ref, k_ref, v_ref, qn_ref, kn_ref, o_ref, qs, ks):
    S = q_ref.shape[0]
    nb = S // KEY_BLOCK
    scale = SB_HEAD_DIM ** -0.5
    qs[...] = (_rms(q_ref[...].astype(F32), qn_ref[...]) * scale).astype(BF16)
    ks[...] = _rms(k_ref[...].astype(F32), kn_ref[...]).astype(BF16)

    rr = lax.broadcasted_iota(jnp.int32, (2 * KEY_BLOCK, 2 * KEY_BLOCK), 0) % KEY_BLOCK
    cc = lax.broadcasted_iota(jnp.int32, (2 * KEY_BLOCK, 2 * KEY_BLOCK), 1)
    suffix = jnp.where((cc >= KEY_BLOCK) | (rr > cc), 1.0, 0.0).astype(BF16)
    qpos = lax.broadcasted_iota(jnp.int32, (KEY_BLOCK, KEY_BLOCK), 0)
    kpos = lax.broadcasted_iota(jnp.int32, (KEY_BLOCK, KEY_BLOCK), 1)
    causal = kpos < qpos

    def block(qb, j, carry, acc, diagonal):
        keys = pl.ds(pl.multiple_of(j * KEY_BLOCK, KEY_BLOCK), KEY_BLOCK)
        z = lax.dot_general(qb, ks[keys, :], (((1,), (1,)), ((), ())),
                            preferred_element_type=F32)
        log_beta = jnp.minimum(z, 0.0) - jnp.log(1.0 + jnp.exp(-jnp.abs(z)))
        log_keep = log_beta - z
        if diagonal:
            log_keep = jnp.where(causal, log_keep, 0.0)
        hi = log_keep.astype(BF16)
        lo = (log_keep - hi.astype(F32)).astype(BF16)
        sums = jnp.dot(jnp.concatenate([hi, lo], axis=1), suffix, preferred_element_type=F32)
        w = jnp.exp(log_beta + sums[:, :KEY_BLOCK] + carry)
        if diagonal:
            w = jnp.where(causal, w, 0.0)
        acc = acc + jnp.dot(w.astype(BF16), v_ref[keys, :], preferred_element_type=F32)
        return carry + sums[:, KEY_BLOCK:], acc

    def q_block(i, _):
        rows = pl.ds(pl.multiple_of(i * KEY_BLOCK, KEY_BLOCK), KEY_BLOCK)
        qb = qs[rows, :]
        zeros = jnp.zeros((KEY_BLOCK, KEY_BLOCK), F32)
        carry, acc = block(qb, i, zeros, zeros, True)

        def earlier(n, state):
            return block(qb, i - 1 - n, state[0], state[1], False)

        _, acc = lax.fori_loop(0, i, earlier, (carry, acc))
        o_ref[rows, :] = acc.astype(BF16)
        return 0

    lax.fori_loop(0, nb, q_block, 0)


def _attn(proj, B, S, qn, kn):
    T = B * S
    blk = lambda col0: pl.BlockSpec((S, SB_HEAD_DIM), lambda b, h: (b, col0 + h))
    return pl.pallas_call(
        _attn_kernel,
        grid=(B, SB_HEADS),
        in_specs=[blk(Q_COL0), blk(K_COL0), blk(V_COL0),
                  _const_spec((1, SB_HEAD_DIM)), _const_spec((1, SB_HEAD_DIM))],
        out_specs=pl.BlockSpec((S, SB_HEAD_DIM), lambda b, h: (b, h)),
        out_shape=jax.ShapeDtypeStruct((T, D_MODEL), BF16),
        scratch_shapes=[pltpu.VMEM((S, SB_HEAD_DIM), BF16), pltpu.VMEM((S, SB_HEAD_DIM), BF16)],
        compiler_params=_cparams("parallel", "parallel"),
        name="attn",
    )(proj, proj, proj, qn, kn)


def _merge_kernel(x_ref, yp_ref, yl_ref, ys_ref, g0_ref, g1_ref, g2_ref, bg_ref, wb_ref,
                  wo_ref, o_ref):
    merged = None
    for n, (y_ref, g_ref) in enumerate(((yp_ref, g0_ref), (yl_ref, g1_ref), (ys_ref, g2_ref))):
        gate = jax.nn.sigmoid(g_ref[...].astype(F32) + bg_ref[n:n + 1, :])
        term = gate * jnp.dot(y_ref[...], wb_ref[n], preferred_element_type=F32)
        merged = term if merged is None else merged + term
    o_ref[...] = x_ref[...] + jnp.dot(merged.astype(BF16), wo_ref[...],
                                      preferred_element_type=F32)


def _merge(x2, yp, yl, ys, proj, bg, wb, wo, tm):
    T = x2.shape[0]
    tile = pl.BlockSpec((tm, D_MODEL), lambda i: (i, 0))
    gate = lambda n: pl.BlockSpec((tm, D_MODEL), lambda i: (i, GATE_COL0 + n))
    return pl.pallas_call(
        _merge_kernel,
        grid=(T // tm,),
        in_specs=[tile, tile, tile, tile, gate(0), gate(1), gate(2),
                  _const_spec((3, D_MODEL)),
                  _const_spec((3, D_MODEL, D_MODEL)),
                  _const_spec((D_MODEL, D_MODEL))],
        out_specs=tile,
        out_shape=jax.ShapeDtypeStruct((T, D_MODEL), F32),
        compiler_params=_cparams("parallel"),
        name="merge",
    )(x2, yp, yl, ys, proj, proj, proj, bg, wb, wo)


def _ffn_kernel(x_ref, g_ref, wg_ref, wu_ref, wd_ref, o_ref):
    x = x_ref[...]
    h = _rms(x, g_ref[...]).astype(BF16)
    gate = jnp.dot(h, wg_ref[...], preferred_element_type=F32)
    up = jnp.dot(h, wu_ref[...], preferred_element_type=F32)
    act = (gate * jax.nn.sigmoid(gate) * up).astype(BF16)
    o_ref[...] = x + jnp.dot(act, wd_ref[...], preferred_element_type=F32)


def _ffn(x2, g, wg, wu, wd, tm):
    T = x2.shape[0]
    tile = pl.BlockSpec((tm, D_MODEL), lambda i: (i, 0))
    return pl.pallas_call(
        _ffn_kernel,
        grid=(T // tm,),
        in_specs=[tile, _const_spec((1, D_MODEL)), _const_spec((D_MODEL, D_FF)),
                  _const_spec((D_MODEL, D_FF)), _const_spec((D_FF, D_MODEL))],
        out_specs=tile,
        out_shape=jax.ShapeDtypeStruct((T, D_MODEL), F32),
        compiler_params=_cparams("parallel"),
        name="ffn",
    )(x2, g, wg, wu, wd)


def _gate_tiles(w_rg, w_ig):
    nt = D_MODEL // CH_TILE
    per = CH_TILE // LRU_BLOCK
    eye = jnp.eye(per, dtype=F32)

    def tiles(w):
        w4 = w.reshape(nt, per, LRU_BLOCK, LRU_BLOCK)
        return jnp.einsum('tbij,bc->tbicj', w4, eye).reshape(nt, CH_TILE, CH_TILE)

    return jnp.concatenate([tiles(w_rg), tiles(w_ig)], axis=-1).astype(BF16)


def kernel(x, norm_mix, w_in, b_gate, w_pool, pool_scale, conv_w, conv_b, w_rg, b_rg, w_ig, b_ig,
           lru_lambda, q_norm, k_norm, w_branch, w_out, norm_ffn, w_ffn_gate, w_ffn_up,
           w_ffn_down):
    B, S, D = x.shape
    assert D == D_MODEL and S % KEY_BLOCK == 0
    T = B * S
    tm = min(512, T)
    assert T % tm == 0
    depth = w_in.shape[0]
    row = lambda v: v.reshape(1, -1)

    x2 = x.reshape(T, D)
    for l in range(depth):
        proj = _inproj(x2, row(norm_mix[l]), w_in[l].astype(BF16), tm)
        yp, yl = _mixers(proj, B, S, w_pool[l].astype(BF16), row(pool_scale[l]), conv_w[l],
                         row(conv_b[l]), _gate_tiles(w_rg[l], w_ig[l]), row(b_rg[l]),
                         row(b_ig[l]), row(lru_lambda[l]))
        ys = _attn(proj, B, S, row(q_norm[l]), row(k_norm[l]))
        x2 = _merge(x2, yp, yl, ys, proj, b_gate[l].reshape(3, D), w_branch[l].astype(BF16),
                    w_out[l].astype(BF16), tm)
        x2 = _ffn(x2, row(norm_ffn[l]), w_ffn_gate[l].astype(BF16), w_ffn_up[l].astype(BF16),
                  w_ffn_down[l].astype(BF16), tm)
    return x2.reshape(B, S, D)
```

```python
import functools

import jax
import jax.numpy as jnp
from jax import lax
from jax.experimental import pallas as pl
from jax.experimental.pallas import tpu as pltpu

F32 = jnp.float32
BF16 = jnp.bfloat16

D_MODEL = 1024
EPS = 1e-6
N_POOL_GROUPS = 4
POOL_GROUP = 256
LRU_BLOCKS = 16
LRU_BLOCK = 64
CONV_WIDTH = 4
LRU_C = 8.0
SB_HEADS = 8
SB_HEAD_DIM = 128
D_FF = 2816
IN_COLS = 8 * D_MODEL

CH_TILE = 256
LRU_COL0 = D_MODEL // CH_TILE
HEADS_PER_STEP = 4
HEAD_COLS = HEADS_PER_STEP * SB_HEAD_DIM
Q_COL0 = 2 * D_MODEL // HEAD_COLS
K_COL0 = 3 * D_MODEL // HEAD_COLS
V_COL0 = 4 * D_MODEL // HEAD_COLS
GATE_COL0 = 5
QK_WIDTH = 2 * D_MODEL
QK_CHUNK = 1
LOG2E = 1.4426950408889634

KEY_BLOCK = 128
Q_TILE = 256
KEY_CHUNK = 512
ROW_TILE = 8
VMEM_LIMIT = 56 * 1024 * 1024


def _cparams(*sem):
    return pltpu.CompilerParams(dimension_semantics=sem, vmem_limit_bytes=VMEM_LIMIT)


def _const_spec(shape):
    zeros = (0,) * len(shape)
    return pl.BlockSpec(shape, lambda *_: zeros, pipeline_mode=pl.Buffered(1))


def _rms(x, g):
    return x * lax.rsqrt(jnp.mean(x * x, axis=-1, keepdims=True) + EPS) * g


def _inproj_kernel(x_ref, g_ref, w_ref, qkg_ref, o_ref):
    h = _rms(x_ref[...], g_ref[...]).astype(BF16)
    for c in range(IN_COLS // QK_WIDTH):
        cols = slice(c * QK_WIDTH, (c + 1) * QK_WIDTH)
        r = jnp.dot(h, w_ref[:, cols], preferred_element_type=F32)
        if c == QK_CHUNK:
            heads = []
            for hd in range(QK_WIDTH // SB_HEAD_DIM):
                seg = r[:, hd * SB_HEAD_DIM:(hd + 1) * SB_HEAD_DIM]
                ms = jnp.mean(seg * seg, axis=-1, keepdims=True)
                heads.append(seg * lax.rsqrt(ms + EPS))
            r = jnp.concatenate(heads, axis=1) * qkg_ref[...]
        o_ref[:, cols] = r.astype(BF16)


def _inproj(x2, g, w, qk_gain, tm):
    T = x2.shape[0]
    return pl.pallas_call(
        _inproj_kernel,
        grid=(T // tm,),
        in_specs=[pl.BlockSpec((tm, D_MODEL), lambda i: (i, 0)),
                  _const_spec((1, D_MODEL)),
                  _const_spec((D_MODEL, IN_COLS)),
                  _const_spec((1, QK_WIDTH))],
        out_specs=pl.BlockSpec((tm, IN_COLS), lambda i: (i, 0)),
        out_shape=jax.ShapeDtypeStruct((T, IN_COLS), BF16),
        compiler_params=_cparams("parallel"),
        name="inproj",
    )(x2, g, w, qk_gain)


def _shift_rows(x, d, row, min_row):
    return jnp.where(row >= min_row, pltpu.roll(x, d, axis=0), 0.0)


def _mixers_kernel(up_ref, ul_ref, wpool_ref, pscale_ref, convw_ref, convb_ref, wgate_ref,
                   brg_ref, big_ref, lam_ref, yp_ref, yl_ref, a_scr, b_scr):
    S = up_ref.shape[0]
    j = pl.program_id(1)
    row = lax.broadcasted_iota(jnp.int32, (S, CH_TILE), 0)

    u = up_ref[...].astype(F32)
    s = u
    for k, d in enumerate((1, 2, 4, 8)):
        min_row = jnp.where(j >= k, d, S)
        s = s + _shift_rows(s, d, row, min_row)
    window = jnp.left_shift(2, j)
    count = jnp.minimum(row + 1, window).astype(F32)
    p = (s / count - u).astype(BF16)
    y = jnp.dot(p, wpool_ref[0], preferred_element_type=F32) * pscale_ref[...]
    yp_ref[...] = y.astype(BF16)

    ul = ul_ref[...].astype(F32)
    xc = convb_ref[...] + convw_ref[CONV_WIDTH - 1:CONV_WIDTH, :] * ul
    for k in range(CONV_WIDTH - 1):
        d = CONV_WIDTH - 1 - k
        xc = xc + convw_ref[k:k + 1, :] * _shift_rows(ul, d, row, d)
    gates = jnp.dot(xc.astype(BF16), wgate_ref[0], preferred_element_type=F32)
    r = jax.nn.sigmoid(gates[:, :CH_TILE] + brg_ref[...])
    ig = jax.nn.sigmoid(gates[:, CH_TILE:] + big_ref[...])
    lam = lam_ref[...]
    softplus_neg_lam = jnp.maximum(-lam, 0.0) + jnp.log1p(jnp.exp(-jnp.abs(lam)))
    log_a = (-LRU_C * r) * softplus_neg_lam
    a = jnp.exp(log_a)
    a_scr[...] = a
    b_scr[...] = jnp.sqrt(-jnp.tanh(log_a) * (a * a + 1.0)) * (ig * xc)

    sub = lax.broadcasted_iota(jnp.int32, (ROW_TILE, CH_TILE), 0)

    def scan_tile(t, h_prev):
        rows = pl.ds(pl.multiple_of(t * ROW_TILE, ROW_TILE), ROW_TILE)
        a = a_scr[rows, :]
        b = b_scr[rows, :]
        for d in (1, 2, 4):
            keep = sub >= d
            b = a * jnp.where(keep, pltpu.roll(b, d, axis=0), 0.0) + b
            a = a * jnp.where(keep, pltpu.roll(a, d, axis=0), 1.0)
        h = a * h_prev + b
        b_scr[rows, :] = h
        return jnp.broadcast_to(h[ROW_TILE - 1:ROW_TILE, :], (ROW_TILE, CH_TILE))

    lax.fori_loop(0, S // ROW_TILE, scan_tile, jnp.zeros((ROW_TILE, CH_TILE), F32), unroll=4)
    yl_ref[...] = b_scr[...].astype(BF16)


def _mixers(proj, B, S, wpool, pscale, convw, convb, wgate, brg, big, lam):
    T = B * S
    nt = D_MODEL // CH_TILE
    vec = pl.BlockSpec((1, CH_TILE), lambda b, j: (0, j))
    return pl.pallas_call(
        _mixers_kernel,
        grid=(B, nt),
        in_specs=[pl.BlockSpec((S, CH_TILE), lambda b, j: (b, j)),
                  pl.BlockSpec((S, CH_TILE), lambda b, j: (b, LRU_COL0 + j)),
                  pl.BlockSpec((1, POOL_GROUP, POOL_GROUP), lambda b, j: (j, 0, 0)),
                  vec,
                  pl.BlockSpec((CONV_WIDTH, CH_TILE), lambda b, j: (0, j)),
                  vec,
                  pl.BlockSpec((1, CH_TILE, 2 * CH_TILE), lambda b, j: (j, 0, 0)),
                  vec, vec, vec],
        out_specs=[pl.BlockSpec((S, CH_TILE), lambda b, j: (b, j)),
                   pl.BlockSpec((S, CH_TILE), lambda b, j: (b, j))],
        out_shape=[jax.ShapeDtypeStruct((T, D_MODEL), BF16),
                   jax.ShapeDtypeStruct((T, D_MODEL), BF16)],
        scratch_shapes=[pltpu.VMEM((S, CH_TILE), F32), pltpu.VMEM((S, CH_TILE), F32)],
        compiler_params=_cparams("parallel", "parallel"),
        name="mixers",
    )(proj, proj, wpool, pscale, convw, convb, wgate, brg, big, lam)


def _attn_kernel(q_ref, k_ref, v_ref, o_ref, *scratch):
    S = q_ref.shape[0]
    carry_scr, acc_scr = scratch[:HEADS_PER_STEP], scratch[HEADS_PER_STEP:]

    rr = lax.broadcasted_iota(jnp.int32, (2 * KEY_BLOCK, 2 * KEY_BLOCK), 0) % KEY_BLOCK
    cc = lax.broadcasted_iota(jnp.int32, (2 * KEY_BLOCK, 2 * KEY_BLOCK), 1)
    suffix = jnp.where((cc >= KEY_BLOCK) | (rr > cc), 1.0, 0.0).astype(BF16)
    qpos = lax.broadcasted_iota(jnp.int32, (Q_TILE, Q_TILE), 0)
    kpos = lax.broadcasted_iota(jnp.int32, (Q_TILE, Q_TILE), 1)
    causal = kpos < qpos
    sign_bit = jnp.uint32(0x80000000)

    def fold(row0, key0, n_keys, diagonal):
        heads = range(HEADS_PER_STEP)
        hcols = [slice(hd * SB_HEAD_DIM, (hd + 1) * SB_HEAD_DIM) for hd in heads]
        keys = pl.ds(pl.multiple_of(key0, Q_TILE), n_keys)
        n_blk = n_keys // KEY_BLOCK
        z = [lax.dot_general(q_ref[pl.ds(row0, Q_TILE), hcols[hd]], k_ref[keys, hcols[hd]],
                             (((1,), (1,)), ((), ())), preferred_element_type=F32)
             for hd in heads]
        log_beta, sums = [], []
        for hd in heads:
            neg_abs = lax.bitcast_convert_type(
                lax.bitcast_convert_type(z[hd], jnp.uint32) | sign_bit, F32)
            lb = jnp.minimum(z[hd], 0.0) - jnp.log(1.0 + jnp.exp2(neg_abs)) * LOG2E
            log_keep = lb - z[hd]
            if diagonal:
                log_keep = jnp.where(causal, log_keep, 0.0)
            hi = log_keep.astype(BF16)
            lo = (log_keep - hi.astype(F32)).astype(BF16)
            log_beta.append(lb)
            sums.append([jnp.dot(jnp.concatenate([hi[:, c * KEY_BLOCK:(c + 1) * KEY_BLOCK],
                                                  lo[:, c * KEY_BLOCK:(c + 1) * KEY_BLOCK]], axis=1),
                                 suffix, preferred_element_type=F32) for c in range(n_blk)])
        for hd in heads:
            carry = carry_scr[hd][...]
            ws = [None] * n_blk
            for c in reversed(range(n_blk)):
                cols = slice(c * KEY_BLOCK, (c + 1) * KEY_BLOCK)
                w = jnp.exp2(log_beta[hd][:, cols] + sums[hd][c][:, :KEY_BLOCK] + carry)
                if diagonal:
                    w = jnp.where(causal[:, cols], w, 0.0)
                ws[c] = w.astype(BF16)
                carry = carry + sums[hd][c][:, KEY_BLOCK:]
            carry_scr[hd][...] = carry
            acc_scr[hd][...] += jnp.dot(jnp.concatenate(ws, axis=1), v_ref[keys, hcols[hd]],
                                        preferred_element_type=F32)

    def q_tile(i, _):
        row0 = pl.multiple_of(i * Q_TILE, Q_TILE)
        for ref in scratch:
            ref[...] = jnp.zeros_like(ref)
        fold(row0, row0, Q_TILE, True)

        @pl.when(i % 2 == 1)
        def _():
            fold(row0, row0 - Q_TILE, Q_TILE, False)

        def earlier(n, _):
            fold(row0, (i // 2 - 1 - n) * KEY_CHUNK, KEY_CHUNK, False)
            return 0

        lax.fori_loop(0, i // 2, earlier, 0)
        for hd in range(HEADS_PER_STEP):
            o_ref[pl.ds(row0, Q_TILE), hd * SB_HEAD_DIM:(hd + 1) * SB_HEAD_DIM] = (
                acc_scr[hd][...].astype(BF16))
        return 0

    lax.fori_loop(0, S // Q_TILE, q_tile, 0)


def _attn(proj, B, S):
    T = B * S
    blk = lambda col0: pl.BlockSpec((S, HEAD_COLS), lambda b, h: (b, col0 + h))
    return pl.pallas_call(
        _attn_kernel,
        grid=(B, SB_HEADS // HEADS_PER_STEP),
        in_specs=[blk(Q_COL0), blk(K_COL0), blk(V_COL0)],
        out_specs=pl.BlockSpec((S, HEAD_COLS), lambda b, h: (b, h)),
        out_shape=jax.ShapeDtypeStruct((T, D_MODEL), BF16),
        scratch_shapes=([pltpu.VMEM((Q_TILE, KEY_BLOCK), F32)] * HEADS_PER_STEP
                        + [pltpu.VMEM((Q_TILE, SB_HEAD_DIM), F32)] * HEADS_PER_STEP),
        compiler_params=_cparams("parallel", "parallel"),
        name="attn",
    )(proj, proj, proj)


def _merge_kernel(x_ref, yp_ref, yl_ref, ys_ref, g0_ref, g1_ref, g2_ref, bg_ref, wb_ref,
                  wo_ref, o_ref):
    merged = None
    for n, (y_ref, g_ref) in enumerate(((yp_ref, g0_ref), (yl_ref, g1_ref), (ys_ref, g2_ref))):
        gate = jax.nn.sigmoid(g_ref[...].astype(F32) + bg_ref[n:n + 1, :])
        term = gate * jnp.dot(y_ref[...], wb_ref[n], preferred_element_type=F32)
        merged = term if merged is None else merged + term
    o_ref[...] = x_ref[...] + jnp.dot(merged.astype(BF16), wo_ref[...],
                                      preferred_element_type=F32)


def _merge(x2, yp, yl, ys, proj, bg, wb, wo, tm):
    T = x2.shape[0]
    tile = pl.BlockSpec((tm, D_MODEL), lambda i: (i, 0))
    gate = lambda n: pl.BlockSpec((tm, D_MODEL), lambda i: (i, GATE_COL0 + n))
    return pl.pallas_call(
        _merge_kernel,
        grid=(T // tm,),
        in_specs=[tile, tile, tile, tile, gate(0), gate(1), gate(2),
                  _const_spec((3, D_MODEL)),
                  _const_spec((3, D_MODEL, D_MODEL)),
                  _const_spec((D_MODEL, D_MODEL))],
        out_specs=tile,
        out_shape=jax.ShapeDtypeStruct((T, D_MODEL), F32),
        compiler_params=_cparams("parallel"),
        name="merge",
    )(x2, yp, yl, ys, proj, proj, proj, bg, wb, wo)


def _ffn_kernel(x_ref, g_ref, wg_ref, wu_ref, wd_ref, o_ref):
    x = x_ref[...]
    h = _rms(x, g_ref[...]).astype(BF16)
    gate = jnp.dot(h, wg_ref[...], preferred_element_type=F32)
    up = jnp.dot(h, wu_ref[...], preferred_element_type=F32)
    act = (gate * jax.nn.sigmoid(gate) * up).astype(BF16)
    o_ref[...] = x + jnp.dot(act, wd_ref[...], preferred_element_type=F32)


def _ffn(x2, g, wg, wu, wd, tm):
    T = x2.shape[0]
    tile = pl.BlockSpec((tm, D_MODEL), lambda i: (i, 0))
    return pl.pallas_call(
        _ffn_kernel,
        grid=(T // tm,),
        in_specs=[tile, _const_spec((1, D_MODEL)), _const_spec((D_MODEL, D_FF)),
                  _const_spec((D_MODEL, D_FF)), _const_spec((D_FF, D_MODEL))],
        out_specs=tile,
        out_shape=jax.ShapeDtypeStruct((T, D_MODEL), F32),
        compiler_params=_cparams("parallel"),
        name="ffn",
    )(x2, g, wg, wu, wd)


def _gate_tiles(w_rg, w_ig):
    nt = D_MODEL // CH_TILE
    per = CH_TILE // LRU_BLOCK
    eye = jnp.eye(per, dtype=F32)

    def tiles(w):
        w4 = w.reshape(nt, per, LRU_BLOCK, LRU_BLOCK)
        return jnp.einsum('tbij,bc->tbicj', w4, eye).reshape(nt, CH_TILE, CH_TILE)

    return jnp.concatenate([tiles(w_rg), tiles(w_ig)], axis=-1).astype(BF16)


def kernel(x, norm_mix, w_in, b_gate, w_pool, pool_scale, conv_w, conv_b, w_rg, b_rg, w_ig, b_ig,
           lru_lambda, q_norm, k_norm, w_branch, w_out, norm_ffn, w_ffn_gate, w_ffn_up,
           w_ffn_down):
    B, S, D = x.shape
    assert D == D_MODEL and S % Q_TILE == 0
    T = B * S
    tm = min(512, T)
    assert T % tm == 0
    depth = w_in.shape[0]
    row = lambda v: v.reshape(1, -1)

    x2 = x.reshape(T, D)
    for l in range(depth):
        q_gain = q_norm[l] * (SB_HEAD_DIM ** -0.5 * LOG2E)
        qk_gain = jnp.concatenate([jnp.tile(q_gain, SB_HEADS), jnp.tile(k_norm[l], SB_HEADS)])
        proj = _inproj(x2, row(norm_mix[l]), w_in[l].astype(BF16), row(qk_gain), tm)
        yp, yl = _mixers(proj, B, S, w_pool[l].astype(BF16), row(pool_scale[l]), conv_w[l],
                         row(conv_b[l]), _gate_tiles(w_rg[l], w_ig[l]), row(b_rg[l]),
                         row(b_ig[l]), row(lru_lambda[l]))
        ys = _attn(proj, B, S)
        x2 = _merge(x2, yp, yl, ys, proj, b_gate[l].reshape(3, D), w_branch[l].astype(BF16),
                    w_out[l].astype(BF16), tm)
        x2 = _ffn(x2, row(norm_ffn[l]), w_ffn_gate[l].astype(BF16), w_ffn_up[l].astype(BF16),
                  w_ffn_down[l].astype(BF16), tm)
    return x2.reshape(B, S, D)
```

```python
import functools

import jax
import jax.numpy as jnp
from jax import lax
from jax.experimental import pallas as pl
from jax.experimental.pallas import tpu as pltpu

F32 = jnp.float32
BF16 = jnp.bfloat16

D_MODEL = 1024
EPS = 1e-6
N_POOL_GROUPS = 4
POOL_GROUP = 256
LRU_BLOCKS = 16
LRU_BLOCK = 64
CONV_WIDTH = 4
LRU_C = 8.0
SB_HEADS = 8
SB_HEAD_DIM = 128
D_FF = 2816
IN_COLS = 8 * D_MODEL

CH_TILE = 256
LRU_COL0 = D_MODEL // CH_TILE
HEADS_PER_STEP = 4
HEAD_COLS = HEADS_PER_STEP * SB_HEAD_DIM
Q_COL0 = 2 * D_MODEL // HEAD_COLS
K_COL0 = 3 * D_MODEL // HEAD_COLS
V_COL0 = 4 * D_MODEL // HEAD_COLS
GATE_COL0 = 5
QK_WIDTH = 2 * D_MODEL
QK_CHUNK = 1
LOG2E = 1.4426950408889634

KEY_BLOCK = 128
Q_TILE = 256
KEY_CHUNK = 512
UNDERFLOW_BITS = 150.0
ROW_TILE = 8
VMEM_LIMIT = 56 * 1024 * 1024


def _cparams(*sem):
    return pltpu.CompilerParams(dimension_semantics=sem, vmem_limit_bytes=VMEM_LIMIT)


def _const_spec(shape):
    zeros = (0,) * len(shape)
    return pl.BlockSpec(shape, lambda *_: zeros, pipeline_mode=pl.Buffered(1))


def _rms(x, g):
    return x * lax.rsqrt(jnp.mean(x * x, axis=-1, keepdims=True) + EPS) * g


def _inproj_kernel(x_ref, g_ref, w_ref, qkg_ref, o_ref):
    h = _rms(x_ref[...], g_ref[...]).astype(BF16)
    for c in range(IN_COLS // QK_WIDTH):
        cols = slice(c * QK_WIDTH, (c + 1) * QK_WIDTH)
        r = jnp.dot(h, w_ref[:, cols], preferred_element_type=F32)
        if c == QK_CHUNK:
            heads = []
            for hd in range(QK_WIDTH // SB_HEAD_DIM):
                seg = r[:, hd * SB_HEAD_DIM:(hd + 1) * SB_HEAD_DIM]
                ms = jnp.mean(seg * seg, axis=-1, keepdims=True)
                heads.append(seg * lax.rsqrt(ms + EPS))
            r = jnp.concatenate(heads, axis=1) * qkg_ref[...]
        o_ref[:, cols] = r.astype(BF16)


def _inproj(x2, g, w, qk_gain, tm):
    T = x2.shape[0]
    return pl.pallas_call(
        _inproj_kernel,
        grid=(T // tm,),
        in_specs=[pl.BlockSpec((tm, D_MODEL), lambda i: (i, 0)),
                  _const_spec((1, D_MODEL)),
                  _const_spec((D_MODEL, IN_COLS)),
                  _const_spec((1, QK_WIDTH))],
        out_specs=pl.BlockSpec((tm, IN_COLS), lambda i: (i, 0)),
        out_shape=jax.ShapeDtypeStruct((T, IN_COLS), BF16),
        compiler_params=_cparams("parallel"),
        name="inproj",
    )(x2, g, w, qk_gain)


def _shift_rows(x, d, row, min_row):
    return jnp.where(row >= min_row, pltpu.roll(x, d, axis=0), 0.0)


def _mixers_kernel(up_ref, ul_ref, wpool_ref, pscale_ref, convw_ref, convb_ref, wgate_ref,
                   brg_ref, big_ref, lam_ref, yp_ref, yl_ref, a_scr, b_scr):
    S = up_ref.shape[0]
    j = pl.program_id(1)
    row = lax.broadcasted_iota(jnp.int32, (S, CH_TILE), 0)

    u = up_ref[...].astype(F32)
    s = u
    for k, d in enumerate((1, 2, 4, 8)):
        min_row = jnp.where(j >= k, d, S)
        s = s + _shift_rows(s, d, row, min_row)
    window = jnp.left_shift(2, j)
    count = jnp.minimum(row + 1, window).astype(F32)
    p = (s / count - u).astype(BF16)
    y = jnp.dot(p, wpool_ref[0], preferred_element_type=F32) * pscale_ref[...]
    yp_ref[...] = y.astype(BF16)

    ul = ul_ref[...].astype(F32)
    xc = convb_ref[...] + convw_ref[CONV_WIDTH - 1:CONV_WIDTH, :] * ul
    for k in range(CONV_WIDTH - 1):
        d = CONV_WIDTH - 1 - k
        xc = xc + convw_ref[k:k + 1, :] * _shift_rows(ul, d, row, d)
    gates = jnp.dot(xc.astype(BF16), wgate_ref[0], preferred_element_type=F32)
    r = jax.nn.sigmoid(gates[:, :CH_TILE] + brg_ref[...])
    ig = jax.nn.sigmoid(gates[:, CH_TILE:] + big_ref[...])
    lam = lam_ref[...]
    softplus_neg_lam = jnp.maximum(-lam, 0.0) + jnp.log1p(jnp.exp(-jnp.abs(lam)))
    log_a = (-LRU_C * r) * softplus_neg_lam
    a = jnp.exp(log_a)
    a_scr[...] = a
    b_scr[...] = jnp.sqrt(-jnp.tanh(log_a) * (a * a + 1.0)) * (ig * xc)

    sub = lax.broadcasted_iota(jnp.int32, (ROW_TILE, CH_TILE), 0)

    def scan_tile(t, h_prev):
        rows = pl.ds(pl.multiple_of(t * ROW_TILE, ROW_TILE), ROW_TILE)
        a = a_scr[rows, :]
        b = b_scr[rows, :]
        for d in (1, 2, 4):
            keep = sub >= d
            b = a * jnp.where(keep, pltpu.roll(b, d, axis=0), 0.0) + b
            a = a * jnp.where(keep, pltpu.roll(a, d, axis=0), 1.0)
        h = a * h_prev + b
        b_scr[rows, :] = h
        return jnp.broadcast_to(h[ROW_TILE - 1:ROW_TILE, :], (ROW_TILE, CH_TILE))

    lax.fori_loop(0, S // ROW_TILE, scan_tile, jnp.zeros((ROW_TILE, CH_TILE), F32), unroll=4)
    yl_ref[...] = b_scr[...].astype(BF16)


def _mixers(proj, B, S, wpool, pscale, convw, convb, wgate, brg, big, lam):
    T = B * S
    nt = D_MODEL // CH_TILE
    vec = pl.BlockSpec((1, CH_TILE), lambda b, j: (0, j))
    return pl.pallas_call(
        _mixers_kernel,
        grid=(B, nt),
        in_specs=[pl.BlockSpec((S, CH_TILE), lambda b, j: (b, j)),
                  pl.BlockSpec((S, CH_TILE), lambda b, j: (b, LRU_COL0 + j)),
                  pl.BlockSpec((1, POOL_GROUP, POOL_GROUP), lambda b, j: (j, 0, 0)),
                  vec,
                  pl.BlockSpec((CONV_WIDTH, CH_TILE), lambda b, j: (0, j)),
                  vec,
                  pl.BlockSpec((1, CH_TILE, 2 * CH_TILE), lambda b, j: (j, 0, 0)),
                  vec, vec, vec],
        out_specs=[pl.BlockSpec((S, CH_TILE), lambda b, j: (b, j)),
                   pl.BlockSpec((S, CH_TILE), lambda b, j: (b, j))],
        out_shape=[jax.ShapeDtypeStruct((T, D_MODEL), BF16),
                   jax.ShapeDtypeStruct((T, D_MODEL), BF16)],
        scratch_shapes=[pltpu.VMEM((S, CH_TILE), F32), pltpu.VMEM((S, CH_TILE), F32)],
        compiler_params=_cparams("parallel", "parallel"),
        name="mixers",
    )(proj, proj, wpool, pscale, convw, convb, wgate, brg, big, lam)


def _attn_kernel(q_ref, k_ref, v_ref, o_ref, *scratch):
    S = q_ref.shape[0]
    carry_scr, acc_scr = scratch[:HEADS_PER_STEP], scratch[HEADS_PER_STEP:]

    rr = lax.broadcasted_iota(jnp.int32, (2 * KEY_BLOCK, 2 * KEY_BLOCK), 0) % KEY_BLOCK
    cc = lax.broadcasted_iota(jnp.int32, (2 * KEY_BLOCK, 2 * KEY_BLOCK), 1)
    suffix = jnp.where((cc >= KEY_BLOCK) | (rr > cc), 1.0, 0.0).astype(BF16)
    qpos = lax.broadcasted_iota(jnp.int32, (Q_TILE, Q_TILE), 0)
    kpos = lax.broadcasted_iota(jnp.int32, (Q_TILE, Q_TILE), 1)
    causal = kpos < qpos

    def fold(row0, key0, n_keys, diagonal):
        heads = range(HEADS_PER_STEP)
        hcols = [slice(hd * SB_HEAD_DIM, (hd + 1) * SB_HEAD_DIM) for hd in heads]
        keys = pl.ds(pl.multiple_of(key0, Q_TILE), n_keys)
        n_blk = n_keys // KEY_BLOCK
        z = [lax.dot_general(q_ref[pl.ds(row0, Q_TILE), hcols[hd]], k_ref[keys, hcols[hd]],
                             (((1,), (1,)), ((), ())), preferred_element_type=F32)
             for hd in heads]
        log_beta, sums = [], []
        for hd in heads:
            lo_z = jnp.minimum(z[hd], 0.0)
            hi_z = jnp.maximum(z[hd], 0.0)
            softplus = jnp.log(1.0 + jnp.exp2(lo_z - hi_z)) * LOG2E
            log_beta.append(lo_z - softplus)
            drop = hi_z + softplus
            if diagonal:
                drop = jnp.where(causal, drop, 0.0)
            hi = drop.astype(BF16)
            lo = (drop - hi.astype(F32)).astype(BF16)
            sums.append([jnp.dot(jnp.concatenate([hi[:, c * KEY_BLOCK:(c + 1) * KEY_BLOCK],
                                                  lo[:, c * KEY_BLOCK:(c + 1) * KEY_BLOCK]], axis=1),
                                 suffix, preferred_element_type=F32) for c in range(n_blk)])
        for hd in heads:
            carry = carry_scr[hd][...]
            ws = [None] * n_blk
            for c in reversed(range(n_blk)):
                cols = slice(c * KEY_BLOCK, (c + 1) * KEY_BLOCK)
                w = jnp.exp2(log_beta[hd][:, cols] - sums[hd][c][:, :KEY_BLOCK] - carry)
                if diagonal:
                    w = jnp.where(causal[:, cols], w, 0.0)
                ws[c] = w.astype(BF16)
                carry = carry + sums[hd][c][:, KEY_BLOCK:]
            carry_scr[hd][...] = carry
            acc_scr[hd][...] += jnp.dot(jnp.concatenate(ws, axis=1), v_ref[keys, hcols[hd]],
                                        preferred_element_type=F32)

    def q_tile(i, _):
        row0 = pl.multiple_of(i * Q_TILE, Q_TILE)
        for ref in scratch:
            ref[...] = jnp.zeros_like(ref)
        fold(row0, row0, Q_TILE, True)

        @pl.when(i % 2 == 1)
        def _():
            fold(row0, row0 - Q_TILE, Q_TILE, False)

        def weights_alive():
            least = functools.reduce(jnp.minimum, [ref[...] for ref in carry_scr])
            return jnp.min(least) < UNDERFLOW_BITS

        def earlier(state):
            n, _ = state
            fold(row0, (i // 2 - 1 - n) * KEY_CHUNK, KEY_CHUNK, False)
            return n + 1, weights_alive()

        lax.while_loop(lambda state: (state[0] < i // 2) & state[1], earlier,
                       (0, weights_alive()))
        for hd in range(HEADS_PER_STEP):
            o_ref[pl.ds(row0, Q_TILE), hd * SB_HEAD_DIM:(hd + 1) * SB_HEAD_DIM] = (
                acc_scr[hd][...].astype(BF16))
        return 0

    lax.fori_loop(0, S // Q_TILE, q_tile, 0)


def _attn(proj, B, S):
    T = B * S
    blk = lambda col0: pl.BlockSpec((S, HEAD_COLS), lambda b, h: (b, col0 + h))
    return pl.pallas_call(
        _attn_kernel,
        grid=(B, SB_HEADS // HEADS_PER_STEP),
        in_specs=[blk(Q_COL0), blk(K_COL0), blk(V_COL0)],
        out_specs=pl.BlockSpec((S, HEAD_COLS), lambda b, h: (b, h)),
        out_shape=jax.ShapeDtypeStruct((T, D_MODEL), BF16),
        scratch_shapes=([pltpu.VMEM((Q_TILE, KEY_BLOCK), F32)] * HEADS_PER_STEP
                        + [pltpu.VMEM((Q_TILE, SB_HEAD_DIM), F32)] * HEADS_PER_STEP),
        compiler_params=_cparams("parallel", "parallel"),
        name="attn",
    )(proj, proj, proj)


def _merge_kernel(x_ref, yp_ref, yl_ref, ys_ref, g0_ref, g1_ref, g2_ref, bg_ref, wb_ref,
                  wo_ref, o_ref):
    merged = None
    for n, (y_ref, g_ref) in enumerate(((yp_ref, g0_ref), (yl_ref, g1_ref), (ys_ref, g2_ref))):
        gate = jax.nn.sigmoid(g_ref[...].astype(F32) + bg_ref[n:n + 1, :])
        term = gate * jnp.dot(y_ref[...], wb_ref[n], preferred_element_type=F32)
        merged = term if merged is None else merged + term
    o_ref[...] = x_ref[...] + jnp.dot(merged.astype(BF16), wo_ref[...],
                                      preferred_element_type=F32)


def _merge(x2, yp, yl, ys, proj, bg, wb, wo, tm):
    T = x2.shape[0]
    tile = pl.BlockSpec((tm, D_MODEL), lambda i: (i, 0))
    gate = lambda n: pl.BlockSpec((tm, D_MODEL), lambda i: (i, GATE_COL0 + n))
    return pl.pallas_call(
        _merge_kernel,
        grid=(T // tm,),
        in_specs=[tile, tile, tile, tile, gate(0), gate(1), gate(2),
                  _const_spec((3, D_MODEL)),
                  _const_spec((3, D_MODEL, D_MODEL)),
                  _const_spec((D_MODEL, D_MODEL))],
        out_specs=tile,
        out_shape=jax.ShapeDtypeStruct((T, D_MODEL), F32),
        compiler_params=_cparams("parallel"),
        name="merge",
    )(x2, yp, yl, ys, proj, proj, proj, bg, wb, wo)


def _ffn_kernel(x_ref, g_ref, wg_ref, wu_ref, wd_ref, o_ref):
    x = x_ref[...]
    h = _rms(x, g_ref[...]).astype(BF16)
    gate = jnp.dot(h, wg_ref[...], preferred_element_type=F32)
    up = jnp.dot(h, wu_ref[...], preferred_element_type=F32)
    act = (gate * jax.nn.sigmoid(gate) * up).astype(BF16)
    o_ref[...] = x + jnp.dot(act, wd_ref[...], preferred_element_type=F32)


def _ffn(x2, g, wg, wu, wd, tm):
    T = x2.shape[0]
    tile = pl.BlockSpec((tm, D_MODEL), lambda i: (i, 0))
    return pl.pallas_call(
        _ffn_kernel,
        grid=(T // tm,),
        in_specs=[tile, _const_spec((1, D_MODEL)), _const_spec((D_MODEL, D_FF)),
                  _const_spec((D_MODEL, D_FF)), _const_spec((D_FF, D_MODEL))],
        out_specs=tile,
        out_shape=jax.ShapeDtypeStruct((T, D_MODEL), F32),
        compiler_params=_cparams("parallel"),
        name="ffn",
    )(x2, g, wg, wu, wd)


def _gate_tiles(w_rg, w_ig):
    nt = D_MODEL // CH_TILE
    per = CH_TILE // LRU_BLOCK
    eye = jnp.eye(per, dtype=F32)

    def tiles(w):
        w4 = w.reshape(nt, per, LRU_BLOCK, LRU_BLOCK)
        return jnp.einsum('tbij,bc->tbicj', w4, eye).reshape(nt, CH_TILE, CH_TILE)

    return jnp.concatenate([tiles(w_rg), tiles(w_ig)], axis=-1).astype(BF16)


def kernel(x, norm_mix, w_in, b_gate, w_pool, pool_scale, conv_w, conv_b, w_rg, b_rg, w_ig, b_ig,
           lru_lambda, q_norm, k_norm, w_branch, w_out, norm_ffn, w_ffn_gate, w_ffn_up,
           w_ffn_down):
    B, S, D = x.shape
    assert D == D_MODEL and S % Q_TILE == 0
    T = B * S
    tm = min(512, T)
    assert T % tm == 0
    depth = w_in.shape[0]
    row = lambda v: v.reshape(1, -1)

    x2 = x.reshape(T, D)
    for l in range(depth):
        q_gain = q_norm[l] * (SB_HEAD_DIM ** -0.5 * LOG2E)
        qk_gain = jnp.concatenate([jnp.tile(q_gain, SB_HEADS), jnp.tile(k_norm[l], SB_HEADS)])
        proj = _inproj(x2, row(norm_mix[l]), w_in[l].astype(BF16), row(qk_gain), tm)
        yp, yl = _mixers(proj, B, S, w_pool[l].astype(BF16), row(pool_scale[l]), conv_w[l],
                         row(conv_b[l]), _gate_tiles(w_rg[l], w_ig[l]), row(b_rg[l]),
                         row(b_ig[l]), row(lru_lambda[l]))
        ys = _attn(proj, B, S)
        x2 = _merge(x2, yp, yl, ys, proj, b_gate[l].reshape(3, D), w_branch[l].astype(BF16),
                    w_out[l].astype(BF16), tm)
        x2 = _ffn(x2, row(norm_ffn[l]), w_ffn_gate[l].astype(BF16), w_ffn_up[l].astype(BF16),
                  w_ffn_down[l].astype(BF16), tm)
    return x2.reshape(B, S, D)
```

```python
import functools

import jax
import jax.numpy as jnp
from jax import lax
from jax.experimental import pallas as pl
from jax.experimental.pallas import tpu as pltpu

F32 = jnp.float32
BF16 = jnp.bfloat16

D_MODEL = 1024
EPS = 1e-6
N_POOL_GROUPS = 4
POOL_GROUP = 256
LRU_BLOCKS = 16
LRU_BLOCK = 64
CONV_WIDTH = 4
LRU_C = 8.0
SB_HEADS = 8
SB_HEAD_DIM = 128
D_FF = 2816
IN_COLS = 8 * D_MODEL

POOL_CHUNK, LRU_CHUNK, Q_CHUNK, K_CHUNK, V_CHUNK = range(5)
PROJ_COLS = IN_COLS - Q_CHUNK * D_MODEL
CH_TILE = 256
POOL_HALO = 16
HEADS_PER_STEP = 4
HEAD_COLS = HEADS_PER_STEP * SB_HEAD_DIM
Q_COL0 = (Q_CHUNK - Q_CHUNK) * D_MODEL // HEAD_COLS
K_COL0 = (K_CHUNK - Q_CHUNK) * D_MODEL // HEAD_COLS
V_COL0 = (V_CHUNK - Q_CHUNK) * D_MODEL // HEAD_COLS
GATE_COL0 = V_CHUNK - Q_CHUNK + 1
LOG2E = 1.4426950408889634

KEY_BLOCK = 128
Q_TILE = 256
KEY_CHUNK = 512
UNDERFLOW_BITS = 150.0
ROW_TILE = 8
VMEM_LIMIT = 56 * 1024 * 1024


def _cparams(*sem):
    return pltpu.CompilerParams(dimension_semantics=sem, vmem_limit_bytes=VMEM_LIMIT)


def _const_spec(shape):
    zeros = (0,) * len(shape)
    return pl.BlockSpec(shape, lambda *_: zeros, pipeline_mode=pl.Buffered(1))


def _rms(x, g):
    return x * lax.rsqrt(jnp.mean(x * x, axis=-1, keepdims=True) + EPS) * g


def _front_kernel(x_ref, g_ref, w_ref, qkg_ref, wpool_ref, pscale_ref, convw_ref, convb_ref,
                  wgate_ref, brg_ref, big_ref, lam_ref, proj_ref, yp_ref, yl_ref,
                  pool_tail, lru_tail, h_state, a_scr, b_scr):
    TM = x_ref.shape[0]
    t = pl.program_id(1)

    @pl.when(t == 0)
    def _():
        pool_tail[...] = jnp.zeros_like(pool_tail)
        lru_tail[...] = jnp.zeros_like(lru_tail)
        h_state[...] = jnp.zeros_like(h_state)

    h = _rms(x_ref[...], g_ref[...]).astype(BF16)

    def project(c):
        return jnp.dot(h, w_ref[:, c * D_MODEL:(c + 1) * D_MODEL], preferred_element_type=F32)

    def head_norm(r, c):
        heads = []
        for hd in range(SB_HEADS):
            seg = r[:, hd * SB_HEAD_DIM:(hd + 1) * SB_HEAD_DIM]
            ms = jnp.mean(seg * seg, axis=-1, keepdims=True)
            heads.append(seg * lax.rsqrt(ms + EPS))
        return jnp.concatenate(heads, axis=1) * qkg_ref[c - Q_CHUNK:c - Q_CHUNK + 1, :]

    def emit(c):
        r = project(c)
        if c in (Q_CHUNK, K_CHUNK):
            r = head_norm(r, c)
        proj_ref[:, (c - Q_CHUNK) * D_MODEL:(c - Q_CHUNK + 1) * D_MODEL] = r.astype(BF16)

    u_pool = project(POOL_CHUNK)
    u_lru = project(LRU_CHUNK)

    pos = t * TM + 1 + lax.broadcasted_iota(jnp.int32, (POOL_HALO, CH_TILE), 0)
    lam = lam_ref[...]
    softplus_neg_lam = jnp.maximum(-lam, 0.0) + jnp.log1p(jnp.exp(-jnp.abs(lam)))

    sub = lax.broadcasted_iota(jnp.int32, (TM // ROW_TILE, ROW_TILE, CH_TILE), 1)
    later_chunks = list(range(Q_CHUNK, IN_COLS // D_MODEL))
    for j in range(D_MODEL // CH_TILE):
        cols = slice(j * CH_TILE, (j + 1) * CH_TILE)

        window = 2 << j
        u = u_pool[:, cols]
        s = jnp.concatenate([pool_tail[:, cols], u], axis=0)
        pool_tail[:, cols] = u[TM - POOL_HALO:]
        for d in (1, 2, 4, 8)[:j + 1]:
            s = s + pltpu.roll(s, d, axis=0)
        s = s[POOL_HALO:]
        head_inv = 1.0 / jnp.minimum(pos, window).astype(F32)
        mean = jnp.concatenate([s[:POOL_HALO] * head_inv, s[POOL_HALO:] * (1.0 / window)], axis=0)
        y = jnp.dot((mean - u).astype(BF16), wpool_ref[j], preferred_element_type=F32)
        yp_ref[:, cols] = (y * pscale_ref[:, cols]).astype(BF16)

        ul = u_lru[:, cols]
        hist = jnp.concatenate([lru_tail[:, cols], ul], axis=0)
        lru_tail[:, cols] = ul[TM - ROW_TILE:]
        xc = convb_ref[:, cols] + convw_ref[CONV_WIDTH - 1:CONV_WIDTH, cols] * ul
        for k in range(CONV_WIDTH - 1):
            shifted = pltpu.roll(hist, CONV_WIDTH - 1 - k, axis=0)[ROW_TILE:]
            xc = xc + convw_ref[k:k + 1, cols] * shifted
        gates = jnp.dot(xc.astype(BF16), wgate_ref[j], preferred_element_type=F32)
        decay = (-0.5 * LRU_C) * softplus_neg_lam[:, cols]
        log_a = decay + decay * jnp.tanh(0.5 * (gates[:, :CH_TILE] + brg_ref[:, cols]))
        ig = 0.5 + 0.5 * jnp.tanh(0.5 * (gates[:, CH_TILE:] + big_ref[:, cols]))
        a = jnp.exp(log_a)
        b = jnp.sqrt(-jnp.tanh(log_a) * (a * a + 1.0)) * (ig * xc)

        a = a.reshape(TM // ROW_TILE, ROW_TILE, CH_TILE)
        b = b.reshape(TM // ROW_TILE, ROW_TILE, CH_TILE)
        for d in (1, 2, 4):
            keep = sub >= d
            b = a * jnp.where(keep, pltpu.roll(b, d, axis=1), 0.0) + b
            a = a * jnp.where(keep, pltpu.roll(a, d, axis=1), 1.0)
        a_scr[:, :, cols] = a
        b_scr[:, :, cols] = b

        n_emit = -(-len(later_chunks) // (D_MODEL // CH_TILE - j))
        for c in later_chunks[:n_emit]:
            emit(c)
        later_chunks = later_chunks[n_emit:]

    def scan_tile(n, h_prev):
        hh = a_scr[n] * h_prev + b_scr[n]
        yl_ref[pl.ds(pl.multiple_of(n * ROW_TILE, ROW_TILE), ROW_TILE), :] = hh.astype(BF16)
        return jnp.broadcast_to(hh[ROW_TILE - 1:ROW_TILE, :], (ROW_TILE, D_MODEL))

    h_state[...] = lax.fori_loop(0, TM // ROW_TILE, scan_tile, h_state[...], unroll=4)


def _front(x2, B, S, g, w, qk_gain, wpool, pscale, convw, convb, wgate, brg, big, lam, tm):
    T = B * S
    nt = S // tm
    tile = lambda width: pl.BlockSpec((tm, width), lambda b, t: (b * nt + t, 0))
    vec = _const_spec((1, D_MODEL))
    return pl.pallas_call(
        _front_kernel,
        grid=(B, nt),
        in_specs=[tile(D_MODEL), vec,
                  _const_spec((D_MODEL, IN_COLS)),
                  _const_spec((2, D_MODEL)),
                  _const_spec((N_POOL_GROUPS, POOL_GROUP, POOL_GROUP)),
                  vec,
                  _const_spec((CONV_WIDTH, D_MODEL)),
                  vec,
                  _const_spec((D_MODEL // CH_TILE, CH_TILE, 2 * CH_TILE)),
                  vec, vec, vec],
        out_specs=[tile(PROJ_COLS), tile(D_MODEL), tile(D_MODEL)],
        out_shape=[jax.ShapeDtypeStruct((T, PROJ_COLS), BF16),
                   jax.ShapeDtypeStruct((T, D_MODEL), BF16),
                   jax.ShapeDtypeStruct((T, D_MODEL), BF16)],
        scratch_shapes=[pltpu.VMEM((POOL_HALO, D_MODEL), F32), pltpu.VMEM((ROW_TILE, D_MODEL), F32),
                        pltpu.VMEM((ROW_TILE, D_MODEL), F32),
                        pltpu.VMEM((tm // ROW_TILE, ROW_TILE, D_MODEL), F32),
                        pltpu.VMEM((tm // ROW_TILE, ROW_TILE, D_MODEL), F32)],
        compiler_params=_cparams("parallel", "arbitrary"),
        name="front",
    )(x2, g, w, qk_gain, wpool, pscale, convw, convb, wgate, brg, big, lam)


def _attn_kernel(q_ref, k_ref, v_ref, o_ref, *scratch):
    S = q_ref.shape[0]
    carry_scr, acc_scr = scratch[:HEADS_PER_STEP], scratch[HEADS_PER_STEP:]

    rr = lax.broadcasted_iota(jnp.int32, (2 * KEY_BLOCK, 2 * KEY_BLOCK), 0) % KEY_BLOCK
    cc = lax.broadcasted_iota(jnp.int32, (2 * KEY_BLOCK, 2 * KEY_BLOCK), 1)
    suffix = jnp.where((cc >= KEY_BLOCK) | (rr > cc), 1.0, 0.0).astype(BF16)
    qpos = lax.broadcasted_iota(jnp.int32, (Q_TILE, Q_TILE), 0)
    kpos = lax.broadcasted_iota(jnp.int32, (Q_TILE, Q_TILE), 1)
    causal = kpos < qpos

    def fold(row0, key0, n_keys, diagonal):
        heads = range(HEADS_PER_STEP)
        hcols = [slice(hd * SB_HEAD_DIM, (hd + 1) * SB_HEAD_DIM) for hd in heads]
        keys = pl.ds(pl.multiple_of(key0, Q_TILE), n_keys)
        n_blk = n_keys // KEY_BLOCK
        z = [lax.dot_general(q_ref[pl.ds(row0, Q_TILE), hcols[hd]], k_ref[keys, hcols[hd]],
                             (((1,), (1,)), ((), ())), preferred_element_type=F32)
             for hd in heads]
        log_beta, sums = [], []
        for hd in heads:
            lo_z = jnp.minimum(z[hd], 0.0)
            hi_z = jnp.maximum(z[hd], 0.0)
            softplus = jnp.log(1.0 + jnp.exp2(lo_z - hi_z)) * LOG2E
            log_beta.append(lo_z - softplus)
            drop = hi_z + softplus
            if diagonal:
                drop = jnp.where(causal, drop, 0.0)
            hi = drop.astype(BF16)
            lo = (drop - hi.astype(F32)).astype(BF16)
            sums.append([jnp.dot(jnp.concatenate([hi[:, c * KEY_BLOCK:(c + 1) * KEY_BLOCK],
                                                  lo[:, c * KEY_BLOCK:(c + 1) * KEY_BLOCK]], axis=1),
                                 suffix, preferred_element_type=F32) for c in range(n_blk)])
        for hd in heads:
            carry = carry_scr[hd][...]
            ws = [None] * n_blk
            for c in reversed(range(n_blk)):
                cols = slice(c * KEY_BLOCK, (c + 1) * KEY_BLOCK)
                w = jnp.exp2(log_beta[hd][:, cols] - sums[hd][c][:, :KEY_BLOCK] - carry)
                if diagonal:
                    w = jnp.where(causal[:, cols], w, 0.0)
                ws[c] = w.astype(BF16)
                carry = carry + sums[hd][c][:, KEY_BLOCK:]
            carry_scr[hd][...] = carry
            acc_scr[hd][...] += jnp.dot(jnp.concatenate(ws, axis=1), v_ref[keys, hcols[hd]],
                                        preferred_element_type=F32)

    def q_tile(i, _):
        row0 = pl.multiple_of(i * Q_TILE, Q_TILE)
        for ref in scratch:
            ref[...] = jnp.zeros_like(ref)
        fold(row0, row0, Q_TILE, True)

        @pl.when(i % 2 == 1)
        def _():
            fold(row0, row0 - Q_TILE, Q_TILE, False)

        def weights_alive():
            least = functools.reduce(jnp.minimum, [ref[...] for ref in carry_scr])
            return jnp.min(least) < UNDERFLOW_BITS

        def earlier(state):
            n, _ = state
            fold(row0, (i // 2 - 1 - n) * KEY_CHUNK, KEY_CHUNK, False)
            return n + 1, weights_alive()

        lax.while_loop(lambda state: (state[0] < i // 2) & state[1], earlier,
                       (0, weights_alive()))
        for hd in range(HEADS_PER_STEP):
            o_ref[pl.ds(row0, Q_TILE), hd * SB_HEAD_DIM:(hd + 1) * SB_HEAD_DIM] = (
                acc_scr[hd][...].astype(BF16))
        return 0

    lax.fori_loop(0, S // Q_TILE, q_tile, 0)


def _attn(proj, B, S):
    T = B * S
    blk = lambda col0: pl.BlockSpec((S, HEAD_COLS), lambda b, h: (b, col0 + h))
    return pl.pallas_call(
        _attn_kernel,
        grid=(B, SB_HEADS // HEADS_PER_STEP),
        in_specs=[blk(Q_COL0), blk(K_COL0), blk(V_COL0)],
        out_specs=pl.BlockSpec((S, HEAD_COLS), lambda b, h: (b, h)),
        out_shape=jax.ShapeDtypeStruct((T, D_MODEL), BF16),
        scratch_shapes=([pltpu.VMEM((Q_TILE, KEY_BLOCK), F32)] * HEADS_PER_STEP
                        + [pltpu.VMEM((Q_TILE, SB_HEAD_DIM), F32)] * HEADS_PER_STEP),
        compiler_params=_cparams("parallel", "parallel"),
        name="attn",
    )(proj, proj, proj)


def _merge_kernel(x_ref, yp_ref, yl_ref, ys_ref, g0_ref, g1_ref, g2_ref, bg_ref, wb_ref,
                  wo_ref, o_ref):
    merged = None
    for n, (y_ref, g_ref) in enumerate(((yp_ref, g0_ref), (yl_ref, g1_ref), (ys_ref, g2_ref))):
        gate = jax.nn.sigmoid(g_ref[...].astype(F32) + bg_ref[n:n + 1, :])
        term = gate * jnp.dot(y_ref[...], wb_ref[n], preferred_element_type=F32)
        merged = term if merged is None else merged + term
    o_ref[...] = x_ref[...] + jnp.dot(merged.astype(BF16), wo_ref[...],
                                      preferred_element_type=F32)


def _merge(x2, yp, yl, ys, proj, bg, wb, wo, tm):
    T = x2.shape[0]
    tile = pl.BlockSpec((tm, D_MODEL), lambda i: (i, 0))
    gate = lambda n: pl.BlockSpec((tm, D_MODEL), lambda i: (i, GATE_COL0 + n))
    return pl.pallas_call(
        _merge_kernel,
        grid=(T // tm,),
        in_specs=[tile, tile, tile, tile, gate(0), gate(1), gate(2),
                  _const_spec((3, D_MODEL)),
                  _const_spec((3, D_MODEL, D_MODEL)),
                  _const_spec((D_MODEL, D_MODEL))],
        out_specs=tile,
        out_shape=jax.ShapeDtypeStruct((T, D_MODEL), F32),
        compiler_params=_cparams("parallel"),
        name="merge",
    )(x2, yp, yl, ys, proj, proj, proj, bg, wb, wo)


def _ffn_kernel(x_ref, g_ref, wg_ref, wu_ref, wd_ref, o_ref):
    x = x_ref[...]
    h = _rms(x, g_ref[...]).astype(BF16)
    gate = jnp.dot(h, wg_ref[...], preferred_element_type=F32)
    up = jnp.dot(h, wu_ref[...], preferred_element_type=F32)
    act = (gate * jax.nn.sigmoid(gate) * up).astype(BF16)
    o_ref[...] = x + jnp.dot(act, wd_ref[...], preferred_element_type=F32)


def _ffn(x2, g, wg, wu, wd, tm):
    T = x2.shape[0]
    tile = pl.BlockSpec((tm, D_MODEL), lambda i: (i, 0))
    return pl.pallas_call(
        _ffn_kernel,
        grid=(T // tm,),
        in_specs=[tile, _const_spec((1, D_MODEL)), _const_spec((D_MODEL, D_FF)),
                  _const_spec((D_MODEL, D_FF)), _const_spec((D_FF, D_MODEL))],
        out_specs=tile,
        out_shape=jax.ShapeDtypeStruct((T, D_MODEL), F32),
        compiler_params=_cparams("parallel"),
        name="ffn",
    )(x2, g, wg, wu, wd)


def _gate_tiles(w_rg, w_ig):
    nt = D_MODEL // CH_TILE
    per = CH_TILE // LRU_BLOCK
    eye = jnp.eye(per, dtype=F32)

    def tiles(w):
        w4 = w.reshape(nt, per, LRU_BLOCK, LRU_BLOCK)
        return jnp.einsum('tbij,bc->tbicj', w4, eye).reshape(nt, CH_TILE, CH_TILE)

    return jnp.concatenate([tiles(w_rg), tiles(w_ig)], axis=-1).astype(BF16)


def kernel(x, norm_mix, w_in, b_gate, w_pool, pool_scale, conv_w, conv_b, w_rg, b_rg, w_ig, b_ig,
           lru_lambda, q_norm, k_norm, w_branch, w_out, norm_ffn, w_ffn_gate, w_ffn_up,
           w_ffn_down):
    B, S, D = x.shape
    assert D == D_MODEL and S % Q_TILE == 0
    T = B * S
    tm = min(512, S)
    assert S % tm == 0
    depth = w_in.shape[0]
    row = lambda v: v.reshape(1, -1)

    x2 = x.reshape(T, D)
    for l in range(depth):
        q_gain = q_norm[l] * (SB_HEAD_DIM ** -0.5 * LOG2E)
        qk_gain = jnp.stack([jnp.tile(q_gain, SB_HEADS), jnp.tile(k_norm[l], SB_HEADS)])
        proj, yp, yl = _front(x2, B, S, row(norm_mix[l]), w_in[l].astype(BF16), qk_gain,
                              w_pool[l].astype(BF16), row(pool_scale[l]), conv_w[l],
                              row(conv_b[l]), _gate_tiles(w_rg[l], w_ig[l]), row(b_rg[l]),
                              row(b_ig[l]), row(lru_lambda[l]), tm)
        ys = _attn(proj, B, S)
        x2 = _merge(x2, yp, yl, ys, proj, b_gate[l].reshape(3, D), w_branch[l].astype(BF16),
                    w_out[l].astype(BF16), tm)
        x2 = _ffn(x2, row(norm_ffn[l]), w_ffn_gate[l].astype(BF16), w_ffn_up[l].astype(BF16),
                  w_ffn_down[l].astype(BF16), tm)
    return x2.reshape(B, S, D)
```

```python
import functools

import jax
import jax.numpy as jnp
from jax import lax
from jax.experimental import pallas as pl
from jax.experimental.pallas import tpu as pltpu

F32 = jnp.float32
BF16 = jnp.bfloat16

D_MODEL = 1024
EPS = 1e-6
N_POOL_GROUPS = 4
POOL_GROUP = 256
LRU_BLOCKS = 16
LRU_BLOCK = 64
CONV_WIDTH = 4
LRU_C = 8.0
SB_HEADS = 8
SB_HEAD_DIM = 128
D_FF = 2816
IN_COLS = 8 * D_MODEL

POOL_CHUNK, LRU_CHUNK, Q_CHUNK, K_CHUNK, V_CHUNK = range(5)
PROJ_COLS = IN_COLS - Q_CHUNK * D_MODEL
CH_TILE = 256
POOL_HALO = 16
HEADS_PER_STEP = 4
HEAD_COLS = HEADS_PER_STEP * SB_HEAD_DIM
Q_COL0 = (Q_CHUNK - Q_CHUNK) * D_MODEL // HEAD_COLS
K_COL0 = (K_CHUNK - Q_CHUNK) * D_MODEL // HEAD_COLS
V_COL0 = (V_CHUNK - Q_CHUNK) * D_MODEL // HEAD_COLS
GATE_COL0 = V_CHUNK - Q_CHUNK + 1
LOG2E = 1.4426950408889634

KEY_BLOCK = 128
Q_TILE = 256
UNDERFLOW_BITS = 150.0
ROW_TILE = 8
VMEM_LIMIT = 56 * 1024 * 1024


def _cparams(*sem):
    return pltpu.CompilerParams(dimension_semantics=sem, vmem_limit_bytes=VMEM_LIMIT)


def _const_spec(shape):
    zeros = (0,) * len(shape)
    return pl.BlockSpec(shape, lambda *_: zeros, pipeline_mode=pl.Buffered(1))


def _rms(x, g):
    return x * lax.rsqrt(jnp.mean(x * x, axis=-1, keepdims=True) + EPS) * g


def _front_kernel(x_ref, g_ref, w_ref, qkg_ref, wpool_ref, pscale_ref, convw_ref, convb_ref,
                  wgate_ref, brg_ref, big_ref, lam_ref, proj_ref, yp_ref, yl_ref,
                  pool_tail, lru_tail, h_state, a_scr, b_scr):
    TM = x_ref.shape[0]
    t = pl.program_id(1)

    @pl.when(t == 0)
    def _():
        pool_tail[...] = jnp.zeros_like(pool_tail)
        lru_tail[...] = jnp.zeros_like(lru_tail)
        h_state[...] = jnp.zeros_like(h_state)

    h = _rms(x_ref[...], g_ref[...]).astype(BF16)

    def project(c):
        return jnp.dot(h, w_ref[:, c * D_MODEL:(c + 1) * D_MODEL], preferred_element_type=F32)

    def head_norm(r, c):
        heads = []
        for hd in range(SB_HEADS):
            seg = r[:, hd * SB_HEAD_DIM:(hd + 1) * SB_HEAD_DIM]
            ms = jnp.mean(seg * seg, axis=-1, keepdims=True)
            heads.append(seg * lax.rsqrt(ms + EPS))
        return jnp.concatenate(heads, axis=1) * qkg_ref[c - Q_CHUNK:c - Q_CHUNK + 1, :]

    def emit(c):
        r = project(c)
        if c in (Q_CHUNK, K_CHUNK):
            r = head_norm(r, c)
        proj_ref[:, (c - Q_CHUNK) * D_MODEL:(c - Q_CHUNK + 1) * D_MODEL] = r.astype(BF16)

    u_pool = project(POOL_CHUNK)
    u_lru = project(LRU_CHUNK)

    pos = t * TM + 1 + lax.broadcasted_iota(jnp.int32, (POOL_HALO, CH_TILE), 0)
    lam = lam_ref[...]
    softplus_neg_lam = jnp.maximum(-lam, 0.0) + jnp.log1p(jnp.exp(-jnp.abs(lam)))

    sub = lax.broadcasted_iota(jnp.int32, (TM // ROW_TILE, ROW_TILE, CH_TILE), 1)
    later_chunks = list(range(Q_CHUNK, IN_COLS // D_MODEL))
    for j in range(D_MODEL // CH_TILE):
        cols = slice(j * CH_TILE, (j + 1) * CH_TILE)

        window = 2 << j
        u = u_pool[:, cols]
        s = jnp.concatenate([pool_tail[:, cols], u], axis=0)
        pool_tail[:, cols] = u[TM - POOL_HALO:]
        for d in (1, 2, 4, 8)[:j + 1]:
            s = s + pltpu.roll(s, d, axis=0)
        s = s[POOL_HALO:]
        head_inv = 1.0 / jnp.minimum(pos, window).astype(F32)
        mean = jnp.concatenate([s[:POOL_HALO] * head_inv, s[POOL_HALO:] * (1.0 / window)], axis=0)
        y = jnp.dot((mean - u).astype(BF16), wpool_ref[j], preferred_element_type=F32)
        yp_ref[:, cols] = (y * pscale_ref[:, cols]).astype(BF16)

        ul = u_lru[:, cols]
        hist = jnp.concatenate([lru_tail[:, cols], ul], axis=0)
        lru_tail[:, cols] = ul[TM - ROW_TILE:]
        xc = convb_ref[:, cols] + convw_ref[CONV_WIDTH - 1:CONV_WIDTH, cols] * ul
        for k in range(CONV_WIDTH - 1):
            shifted = pltpu.roll(hist, CONV_WIDTH - 1 - k, axis=0)[ROW_TILE:]
            xc = xc + convw_ref[k:k + 1, cols] * shifted
        gates = jnp.dot(xc.astype(BF16), wgate_ref[j], preferred_element_type=F32)
        decay = (-0.5 * LRU_C) * softplus_neg_lam[:, cols]
        log_a = decay + decay * jnp.tanh(0.5 * (gates[:, :CH_TILE] + brg_ref[:, cols]))
        ig = 0.5 + 0.5 * jnp.tanh(0.5 * (gates[:, CH_TILE:] + big_ref[:, cols]))
        a = jnp.exp(log_a)
        b = jnp.sqrt(-jnp.tanh(log_a) * (a * a + 1.0)) * (ig * xc)

        a = a.reshape(TM // ROW_TILE, ROW_TILE, CH_TILE)
        b = b.reshape(TM // ROW_TILE, ROW_TILE, CH_TILE)
        for d in (1, 2, 4):
            keep = sub >= d
            b = a * jnp.where(keep, pltpu.roll(b, d, axis=1), 0.0) + b
            a = a * jnp.where(keep, pltpu.roll(a, d, axis=1), 1.0)
        a_scr[:, :, cols] = a
        b_scr[:, :, cols] = b

        n_emit = -(-len(later_chunks) // (D_MODEL // CH_TILE - j))
        for c in later_chunks[:n_emit]:
            emit(c)
        later_chunks = later_chunks[n_emit:]

    def scan_tile(n, h_prev):
        hh = a_scr[n] * h_prev + b_scr[n]
        yl_ref[pl.ds(pl.multiple_of(n * ROW_TILE, ROW_TILE), ROW_TILE), :] = hh.astype(BF16)
        return jnp.broadcast_to(hh[ROW_TILE - 1:ROW_TILE, :], (ROW_TILE, D_MODEL))

    h_state[...] = lax.fori_loop(0, TM // ROW_TILE, scan_tile, h_state[...], unroll=4)


def _front(x2, B, S, g, w, qk_gain, wpool, pscale, convw, convb, wgate, brg, big, lam, tm):
    T = B * S
    nt = S // tm
    tile = lambda width: pl.BlockSpec((tm, width), lambda b, t: (b * nt + t, 0))
    vec = _const_spec((1, D_MODEL))
    return pl.pallas_call(
        _front_kernel,
        grid=(B, nt),
        in_specs=[tile(D_MODEL), vec,
                  _const_spec((D_MODEL, IN_COLS)),
                  _const_spec((2, D_MODEL)),
                  _const_spec((N_POOL_GROUPS, POOL_GROUP, POOL_GROUP)),
                  vec,
                  _const_spec((CONV_WIDTH, D_MODEL)),
                  vec,
                  _const_spec((D_MODEL // CH_TILE, CH_TILE, 2 * CH_TILE)),
                  vec, vec, vec],
        out_specs=[tile(PROJ_COLS), tile(D_MODEL), tile(D_MODEL)],
        out_shape=[jax.ShapeDtypeStruct((T, PROJ_COLS), BF16),
                   jax.ShapeDtypeStruct((T, D_MODEL), BF16),
                   jax.ShapeDtypeStruct((T, D_MODEL), BF16)],
        scratch_shapes=[pltpu.VMEM((POOL_HALO, D_MODEL), F32), pltpu.VMEM((ROW_TILE, D_MODEL), F32),
                        pltpu.VMEM((ROW_TILE, D_MODEL), F32),
                        pltpu.VMEM((tm // ROW_TILE, ROW_TILE, D_MODEL), F32),
                        pltpu.VMEM((tm // ROW_TILE, ROW_TILE, D_MODEL), F32)],
        compiler_params=_cparams("parallel", "arbitrary"),
        name="front",
    )(x2, g, w, qk_gain, wpool, pscale, convw, convb, wgate, brg, big, lam)


def _attn_kernel(q_ref, k_ref, v_ref, o_ref, *scratch):
    S = q_ref.shape[0]
    carry_scr, acc_scr = scratch[:HEADS_PER_STEP], scratch[HEADS_PER_STEP:]

    rr = lax.broadcasted_iota(jnp.int32, (2 * KEY_BLOCK, 2 * KEY_BLOCK), 0) % KEY_BLOCK
    cc = lax.broadcasted_iota(jnp.int32, (2 * KEY_BLOCK, 2 * KEY_BLOCK), 1)
    suffix = jnp.where((cc >= KEY_BLOCK) | (rr > cc), 1.0, 0.0).astype(BF16)
    heads = range(HEADS_PER_STEP)
    hcols = [slice(hd * SB_HEAD_DIM, (hd + 1) * SB_HEAD_DIM) for hd in heads]

    def fold(jobs):
        def visible(job):
            _, _, n_rows, _, n_keys, lead = job
            if lead is None:
                return None
            r = lax.broadcasted_iota(jnp.int32, (n_rows, n_keys), 0)
            c = lax.broadcasted_iota(jnp.int32, (n_rows, n_keys), 1)
            return c < r + lead

        work = [(job, visible(job), hd) for job in jobs for hd in heads]
        z = [lax.dot_general(q_ref[pl.ds(row0 + sub0, n_rows), hcols[hd]],
                             k_ref[pl.ds(pl.multiple_of(key0, Q_TILE), n_keys), hcols[hd]],
                             (((1,), (1,)), ((), ())), preferred_element_type=F32)
             for (row0, sub0, n_rows, key0, n_keys, _), _, hd in work]
        log_beta, sums = [], []
        for (job, mask, hd), zz in zip(work, z):
            lo_z = jnp.minimum(zz, 0.0)
            hi_z = jnp.maximum(zz, 0.0)
            softplus = jnp.log(1.0 + jnp.exp2(lo_z - hi_z)) * LOG2E
            log_beta.append(lo_z - softplus)
            drop = hi_z + softplus
            if mask is not None:
                drop = jnp.where(mask, drop, 0.0)
            hi = drop.astype(BF16)
            lo = (drop - hi.astype(F32)).astype(BF16)
            sums.append([jnp.dot(jnp.concatenate([hi[:, c * KEY_BLOCK:(c + 1) * KEY_BLOCK],
                                                  lo[:, c * KEY_BLOCK:(c + 1) * KEY_BLOCK]], axis=1),
                                 suffix, preferred_element_type=F32)
                         for c in range(job[4] // KEY_BLOCK)])
        for n, (job, mask, hd) in enumerate(work):
            _, sub0, n_rows, key0, n_keys, _ = job
            rows = slice(sub0, sub0 + n_rows)
            n_blk = n_keys // KEY_BLOCK
            carry = carry_scr[hd][rows, :]
            ws = [None] * n_blk
            for c in reversed(range(n_blk)):
                cols = slice(c * KEY_BLOCK, (c + 1) * KEY_BLOCK)
                w = jnp.exp2(log_beta[n][:, cols] - sums[n][c][:, :KEY_BLOCK] - carry)
                if mask is not None:
                    w = jnp.where(mask[:, cols], w, 0.0)
                ws[c] = w.astype(BF16)
                carry = carry + sums[n][c][:, KEY_BLOCK:]
            carry_scr[hd][rows, :] = carry
            acc_scr[hd][rows, :] += jnp.dot(
                jnp.concatenate(ws, axis=1),
                v_ref[pl.ds(pl.multiple_of(key0, Q_TILE), n_keys), hcols[hd]],
                preferred_element_type=F32)

    def q_tile(i, _):
        row0 = pl.multiple_of(i * Q_TILE, Q_TILE)
        for ref in scratch:
            ref[...] = jnp.zeros_like(ref)
        half = Q_TILE // 2
        fold([(row0, 0, half, row0, half, 0), (row0, half, half, row0, Q_TILE, half)])

        def weights_alive():
            least = functools.reduce(jnp.minimum, [ref[...] for ref in carry_scr])
            return jnp.min(least) < UNDERFLOW_BITS

        def earlier(state):
            n, _ = state
            fold([(row0, 0, Q_TILE, row0 - (n + 1) * Q_TILE, Q_TILE, None)])
            return n + 1, weights_alive()

        lax.while_loop(lambda state: (state[0] < i) & state[1], earlier,
                       (jnp.int32(0), jnp.bool_(True)))
        for hd in heads:
            o_ref[pl.ds(row0, Q_TILE), hcols[hd]] = acc_scr[hd][...].astype(BF16)
        return 0

    lax.fori_loop(0, S // Q_TILE, q_tile, 0)


def _attn(proj, B, S):
    T = B * S
    blk = lambda col0: pl.BlockSpec((S, HEAD_COLS), lambda b, h: (b, col0 + h))
    return pl.pallas_call(
        _attn_kernel,
        grid=(B, SB_HEADS // HEADS_PER_STEP),
        in_specs=[blk(Q_COL0), blk(K_COL0), blk(V_COL0)],
        out_specs=pl.BlockSpec((S, HEAD_COLS), lambda b, h: (b, h)),
        out_shape=jax.ShapeDtypeStruct((T, D_MODEL), BF16),
        scratch_shapes=([pltpu.VMEM((Q_TILE, KEY_BLOCK), F32)] * HEADS_PER_STEP
                        + [pltpu.VMEM((Q_TILE, SB_HEAD_DIM), F32)] * HEADS_PER_STEP),
        compiler_params=_cparams("parallel", "parallel"),
        name="attn",
    )(proj, proj, proj)


def _merge_kernel(x_ref, yp_ref, yl_ref, ys_ref, g0_ref, g1_ref, g2_ref, bg_ref, wb_ref,
                  wo_ref, o_ref):
    merged = None
    for n, (y_ref, g_ref) in enumerate(((yp_ref, g0_ref), (yl_ref, g1_ref), (ys_ref, g2_ref))):
        gate = jax.nn.sigmoid(g_ref[...].astype(F32) + bg_ref[n:n + 1, :])
        term = gate * jnp.dot(y_ref[...], wb_ref[n], preferred_element_type=F32)
        merged = term if merged is None else merged + term
    o_ref[...] = x_ref[...] + jnp.dot(merged.astype(BF16), wo_ref[...],
                                      preferred_element_type=F32)


def _merge(x2, yp, yl, ys, proj, bg, wb, wo, tm):
    T = x2.shape[0]
    tile = pl.BlockSpec((tm, D_MODEL), lambda i: (i, 0))
    gate = lambda n: pl.BlockSpec((tm, D_MODEL), lambda i: (i, GATE_COL0 + n))
    return pl.pallas_call(
        _merge_kernel,
        grid=(T // tm,),
        in_specs=[tile, tile, tile, tile, gate(0), gate(1), gate(2),
                  _const_spec((3, D_MODEL)),
                  _const_spec((3, D_MODEL, D_MODEL)),
                  _const_spec((D_MODEL, D_MODEL))],
        out_specs=tile,
        out_shape=jax.ShapeDtypeStruct((T, D_MODEL), F32),
        compiler_params=_cparams("parallel"),
        name="merge",
    )(x2, yp, yl, ys, proj, proj, proj, bg, wb, wo)


def _ffn_kernel(x_ref, g_ref, wg_ref, wu_ref, wd_ref, o_ref):
    x = x_ref[...]
    h = _rms(x, g_ref[...]).astype(BF16)
    gate = jnp.dot(h, wg_ref[...], preferred_element_type=F32)
    up = jnp.dot(h, wu_ref[...], preferred_element_type=F32)
    act = (gate * jax.nn.sigmoid(gate) * up).astype(BF16)
    o_ref[...] = x + jnp.dot(act, wd_ref[...], preferred_element_type=F32)


def _ffn(x2, g, wg, wu, wd, tm):
    T = x2.shape[0]
    tile = pl.BlockSpec((tm, D_MODEL), lambda i: (i, 0))
    return pl.pallas_call(
        _ffn_kernel,
        grid=(T // tm,),
        in_specs=[tile, _const_spec((1, D_MODEL)), _const_spec((D_MODEL, D_FF)),
                  _const_spec((D_MODEL, D_FF)), _const_spec((D_FF, D_MODEL))],
        out_specs=tile,
        out_shape=jax.ShapeDtypeStruct((T, D_MODEL), F32),
        compiler_params=_cparams("parallel"),
        name="ffn",
    )(x2, g, wg, wu, wd)


def _gate_tiles(w_rg, w_ig):
    nt = D_MODEL // CH_TILE
    per = CH_TILE // LRU_BLOCK
    eye = jnp.eye(per, dtype=F32)

    def tiles(w):
        w4 = w.reshape(nt, per, LRU_BLOCK, LRU_BLOCK)
        return jnp.einsum('tbij,bc->tbicj', w4, eye).reshape(nt, CH_TILE, CH_TILE)

    return jnp.concatenate([tiles(w_rg), tiles(w_ig)], axis=-1).astype(BF16)


def kernel(x, norm_mix, w_in, b_gate, w_pool, pool_scale, conv_w, conv_b, w_rg, b_rg, w_ig, b_ig,
           lru_lambda, q_norm, k_norm, w_branch, w_out, norm_ffn, w_ffn_gate, w_ffn_up,
           w_ffn_down):
    B, S, D = x.shape
    assert D == D_MODEL and S % Q_TILE == 0
    T = B * S
    tm = min(512, S)
    assert S % tm == 0
    depth = w_in.shape[0]
    row = lambda v: v.reshape(1, -1)

    x2 = x.reshape(T, D)
    for l in range(depth):
        q_gain = q_norm[l] * (SB_HEAD_DIM ** -0.5 * LOG2E)
        qk_gain = jnp.stack([jnp.tile(q_gain, SB_HEADS), jnp.tile(k_norm[l], SB_HEADS)])
        proj, yp, yl = _front(x2, B, S, row(norm_mix[l]), w_in[l].astype(BF16), qk_gain,
                              w_pool[l].astype(BF16), row(pool_scale[l]), conv_w[l],
                              row(conv_b[l]), _gate_tiles(w_rg[l], w_ig[l]), row(b_rg[l]),
                              row(b_ig[l]), row(lru_lambda[l]), tm)
        ys = _attn(proj, B, S)
        x2 = _merge(x2, yp, yl, ys, proj, b_gate[l].reshape(3, D), w_branch[l].astype(BF16),
                    w_out[l].astype(BF16), tm)
        x2 = _ffn(x2, row(norm_ffn[l]), w_ffn_gate[l].astype(BF16), w_ffn_up[l].astype(BF16),
                  w_ffn_down[l].astype(BF16), tm)
    return x2.reshape(B, S, D)
```

```python
import functools

import jax
import jax.numpy as jnp
from jax import lax
from jax.experimental import pallas as pl
from jax.experimental.pallas import tpu as pltpu

F32 = jnp.float32
BF16 = jnp.bfloat16

D_MODEL = 1024
EPS = 1e-6
N_POOL_GROUPS = 4
POOL_GROUP = 256
LRU_BLOCKS = 16
LRU_BLOCK = 64
CONV_WIDTH = 4
LRU_C = 8.0
SB_HEADS = 8
SB_HEAD_DIM = 128
D_FF = 2816
IN_COLS = 8 * D_MODEL

POOL_CHUNK, LRU_CHUNK, Q_CHUNK, K_CHUNK, V_CHUNK = range(5)
PROJ_COLS = IN_COLS - Q_CHUNK * D_MODEL
CH_TILE = 256
POOL_HALO = 16
HEADS_PER_STEP = 4
HEAD_COLS = HEADS_PER_STEP * SB_HEAD_DIM
Q_COL0 = (Q_CHUNK - Q_CHUNK) * D_MODEL // HEAD_COLS
K_COL0 = (K_CHUNK - Q_CHUNK) * D_MODEL // HEAD_COLS
V_COL0 = (V_CHUNK - Q_CHUNK) * D_MODEL // HEAD_COLS
GATE_COL0 = V_CHUNK - Q_CHUNK + 1
LOG2E = 1.4426950408889634

KEY_BLOCK = 128
Q_TILE = 256
UNDERFLOW_BITS = 150.0
ROW_TILE = 8
VMEM_LIMIT = 56 * 1024 * 1024


def _cparams(*sem):
    return pltpu.CompilerParams(dimension_semantics=sem, vmem_limit_bytes=VMEM_LIMIT)


def _const_spec(shape, layer=None):
    zeros = (0,) * len(shape)
    if layer is None:
        return pl.BlockSpec(shape, lambda *_: zeros, pipeline_mode=pl.Buffered(1))
    return pl.BlockSpec((None,) + tuple(shape), lambda *_: (layer,) + zeros,
                        pipeline_mode=pl.Buffered(1))


def _rms(x, g):
    return x * lax.rsqrt(jnp.mean(x * x, axis=-1, keepdims=True) + EPS) * g


def _front_kernel(x_ref, g_ref, w_ref, qkg_ref, wpool_ref, pscale_ref, convw_ref, convb_ref,
                  wgate_ref, brg_ref, big_ref, lam_ref, proj_ref, yp_ref, yl_ref,
                  pool_tail, lru_tail, h_state, a_scr, b_scr):
    TM = x_ref.shape[0]
    t = pl.program_id(1)

    @pl.when(t == 0)
    def _():
        pool_tail[...] = jnp.zeros_like(pool_tail)
        lru_tail[...] = jnp.zeros_like(lru_tail)
        h_state[...] = jnp.zeros_like(h_state)

    h = _rms(x_ref[...], g_ref[...]).astype(BF16)

    def project(c):
        return jnp.dot(h, w_ref[:, c * D_MODEL:(c + 1) * D_MODEL], preferred_element_type=F32)

    def head_norm(r, c):
        heads = []
        for hd in range(SB_HEADS):
            seg = r[:, hd * SB_HEAD_DIM:(hd + 1) * SB_HEAD_DIM]
            ms = jnp.mean(seg * seg, axis=-1, keepdims=True)
            heads.append(seg * lax.rsqrt(ms + EPS))
        return jnp.concatenate(heads, axis=1) * qkg_ref[c - Q_CHUNK:c - Q_CHUNK + 1, :]

    def emit(c):
        r = project(c)
        if c in (Q_CHUNK, K_CHUNK):
            r = head_norm(r, c)
        proj_ref[:, (c - Q_CHUNK) * D_MODEL:(c - Q_CHUNK + 1) * D_MODEL] = r.astype(BF16)

    u_pool = project(POOL_CHUNK)
    u_lru = project(LRU_CHUNK)

    pos = t * TM + 1 + lax.broadcasted_iota(jnp.int32, (POOL_HALO, CH_TILE), 0)
    lam = lam_ref[...]
    softplus_neg_lam = jnp.maximum(-lam, 0.0) + jnp.log1p(jnp.exp(-jnp.abs(lam)))

    sub = lax.broadcasted_iota(jnp.int32, (TM // ROW_TILE, ROW_TILE, CH_TILE), 1)
    later_chunks = list(range(Q_CHUNK, IN_COLS // D_MODEL))
    for j in range(D_MODEL // CH_TILE):
        cols = slice(j * CH_TILE, (j + 1) * CH_TILE)

        window = 2 << j
        u = u_pool[:, cols]
        s = jnp.concatenate([pool_tail[:, cols], u], axis=0)
        pool_tail[:, cols] = u[TM - POOL_HALO:]
        for d in (1, 2, 4, 8)[:j + 1]:
            s = s + pltpu.roll(s, d, axis=0)
        s = s[POOL_HALO:]
        head_inv = 1.0 / jnp.minimum(pos, window).astype(F32)
        mean = jnp.concatenate([s[:POOL_HALO] * head_inv, s[POOL_HALO:] * (1.0 / window)], axis=0)
        y = jnp.dot((mean - u).astype(BF16), wpool_ref[j], preferred_element_type=F32)
        yp_ref[:, cols] = (y * pscale_ref[:, cols]).astype(BF16)

        ul = u_lru[:, cols]
        hist = jnp.concatenate([lru_tail[:, cols], ul], axis=0)
        lru_tail[:, cols] = ul[TM - ROW_TILE:]
        xc = convb_ref[:, cols] + convw_ref[CONV_WIDTH - 1:CONV_WIDTH, cols] * ul
        for k in range(CONV_WIDTH - 1):
            shifted = pltpu.roll(hist, CONV_WIDTH - 1 - k, axis=0)[ROW_TILE:]
            xc = xc + convw_ref[k:k + 1, cols] * shifted
        gates = jnp.dot(xc.astype(BF16), wgate_ref[j], preferred_element_type=F32)
        decay = (-0.5 * LRU_C) * softplus_neg_lam[:, cols]
        log_a = decay + decay * jnp.tanh(0.5 * (gates[:, :CH_TILE] + brg_ref[:, cols]))
        ig = 0.5 + 0.5 * jnp.tanh(0.5 * (gates[:, CH_TILE:] + big_ref[:, cols]))
        a = jnp.exp(log_a)
        b = jnp.sqrt(-jnp.tanh(log_a) * (a * a + 1.0)) * (ig * xc)

        a = a.reshape(TM // ROW_TILE, ROW_TILE, CH_TILE)
        b = b.reshape(TM // ROW_TILE, ROW_TILE, CH_TILE)
        for d in (1, 2, 4):
            keep = sub >= d
            b = a * jnp.where(keep, pltpu.roll(b, d, axis=1), 0.0) + b
            a = a * jnp.where(keep, pltpu.roll(a, d, axis=1), 1.0)
        a_scr[:, :, cols] = a
        b_scr[:, :, cols] = b

        n_emit = -(-len(later_chunks) // (D_MODEL // CH_TILE - j))
        for c in later_chunks[:n_emit]:
            emit(c)
        later_chunks = later_chunks[n_emit:]

    def scan_tile(n, h_prev):
        hh = a_scr[n] * h_prev + b_scr[n]
        yl_ref[pl.ds(pl.multiple_of(n * ROW_TILE, ROW_TILE), ROW_TILE), :] = hh.astype(BF16)
        return jnp.broadcast_to(hh[ROW_TILE - 1:ROW_TILE, :], (ROW_TILE, D_MODEL))

    h_state[...] = lax.fori_loop(0, TM // ROW_TILE, scan_tile, h_state[...], unroll=4)


def _front(x2, B, S, layer, g, w, qk_gain, wpool, pscale, convw, convb, wgate, brg, big, lam, tm):
    T = B * S
    nt = S // tm
    tile = lambda width: pl.BlockSpec((tm, width), lambda b, t: (b * nt + t, 0))
    vec = _const_spec((1, D_MODEL))
    return pl.pallas_call(
        _front_kernel,
        grid=(B, nt),
        in_specs=[tile(D_MODEL), vec,
                  _const_spec((D_MODEL, IN_COLS), layer),
                  _const_spec((2, D_MODEL)),
                  _const_spec((N_POOL_GROUPS, POOL_GROUP, POOL_GROUP), layer),
                  vec,
                  _const_spec((CONV_WIDTH, D_MODEL)),
                  vec,
                  _const_spec((D_MODEL // CH_TILE, CH_TILE, 2 * CH_TILE), layer),
                  vec, vec, vec],
        out_specs=[tile(PROJ_COLS), tile(D_MODEL), tile(D_MODEL)],
        out_shape=[jax.ShapeDtypeStruct((T, PROJ_COLS), BF16),
                   jax.ShapeDtypeStruct((T, D_MODEL), BF16),
                   jax.ShapeDtypeStruct((T, D_MODEL), BF16)],
        scratch_shapes=[pltpu.VMEM((POOL_HALO, D_MODEL), F32), pltpu.VMEM((ROW_TILE, D_MODEL), F32),
                        pltpu.VMEM((ROW_TILE, D_MODEL), F32),
                        pltpu.VMEM((tm // ROW_TILE, ROW_TILE, D_MODEL), F32),
                        pltpu.VMEM((tm // ROW_TILE, ROW_TILE, D_MODEL), F32)],
        compiler_params=_cparams("parallel", "arbitrary"),
        name="front",
    )(x2, g, w, qk_gain, wpool, pscale, convw, convb, wgate, brg, big, lam)


def _attn_kernel(q_ref, k_ref, v_ref, o_ref, *scratch):
    S = q_ref.shape[0]
    carry_scr, acc_scr = scratch[:HEADS_PER_STEP], scratch[HEADS_PER_STEP:]

    rr = lax.broadcasted_iota(jnp.int32, (2 * KEY_BLOCK, 2 * KEY_BLOCK), 0) % KEY_BLOCK
    cc = lax.broadcasted_iota(jnp.int32, (2 * KEY_BLOCK, 2 * KEY_BLOCK), 1)
    suffix = jnp.where((cc >= KEY_BLOCK) | (rr > cc), 1.0, 0.0).astype(BF16)
    heads = range(HEADS_PER_STEP)
    hcols = [slice(hd * SB_HEAD_DIM, (hd + 1) * SB_HEAD_DIM) for hd in heads]

    def fold(jobs):
        def visible(job):
            _, _, n_rows, _, n_keys, lead = job
            if lead is None:
                return None
            r = lax.broadcasted_iota(jnp.int32, (n_rows, n_keys), 0)
            c = lax.broadcasted_iota(jnp.int32, (n_rows, n_keys), 1)
            return c < r + lead

        work = [(job, visible(job), hd) for job in jobs for hd in heads]
        z = [lax.dot_general(q_ref[pl.ds(row0 + sub0, n_rows), hcols[hd]],
                             k_ref[pl.ds(pl.multiple_of(key0, Q_TILE), n_keys), hcols[hd]],
                             (((1,), (1,)), ((), ())), preferred_element_type=F32)
             for (row0, sub0, n_rows, key0, n_keys, _), _, hd in work]
        log_beta, sums = [], []
        for (job, mask, hd), zz in zip(work, z):
            lo_z = jnp.minimum(zz, 0.0)
            hi_z = jnp.maximum(zz, 0.0)
            softplus = jnp.log(1.0 + jnp.exp2(lo_z - hi_z)) * LOG2E
            log_beta.append(lo_z - softplus)
            drop = hi_z + softplus
            if mask is not None:
                drop = jnp.where(mask, drop, 0.0)
            hi = drop.astype(BF16)
            lo = (drop - hi.astype(F32)).astype(BF16)
            sums.append([jnp.dot(jnp.concatenate([hi[:, c * KEY_BLOCK:(c + 1) * KEY_BLOCK],
                                                  lo[:, c * KEY_BLOCK:(c + 1) * KEY_BLOCK]], axis=1),
                                 suffix, preferred_element_type=F32)
                         for c in range(job[4] // KEY_BLOCK)])
        for n, (job, mask, hd) in enumerate(work):
            _, sub0, n_rows, key0, n_keys, _ = job
            rows = slice(sub0, sub0 + n_rows)
            n_blk = n_keys // KEY_BLOCK
            carry = carry_scr[hd][rows, :]
            ws = [None] * n_blk
            for c in reversed(range(n_blk)):
                cols = slice(c * KEY_BLOCK, (c + 1) * KEY_BLOCK)
                w = jnp.exp2(log_beta[n][:, cols] - sums[n][c][:, :KEY_BLOCK] - carry)
                if mask is not None:
                    w = jnp.where(mask[:, cols], w, 0.0)
                ws[c] = w.astype(BF16)
                carry = carry + sums[n][c][:, KEY_BLOCK:]
            carry_scr[hd][rows, :] = carry
            acc_scr[hd][rows, :] += jnp.dot(
                jnp.concatenate(ws, axis=1),
                v_ref[pl.ds(pl.multiple_of(key0, Q_TILE), n_keys), hcols[hd]],
                preferred_element_type=F32)

    def q_tile(i, first):
        row0 = pl.multiple_of(i * Q_TILE, Q_TILE)
        for ref in scratch:
            ref[...] = jnp.zeros_like(ref)

        def chunk(n):
            return (row0, 0, Q_TILE, row0 - n * Q_TILE, Q_TILE, None)

        half = Q_TILE // 2
        diagonal = [(row0, 0, half, row0, half, 0), (row0, half, half, row0, Q_TILE, half)]
        if first:
            fold(diagonal)
        else:
            fold(diagonal + [chunk(1)])

            def weights_alive():
                least = functools.reduce(jnp.minimum, [ref[...] for ref in carry_scr])
                return jnp.min(least) < UNDERFLOW_BITS

            def earlier(state):
                n, _ = state
                fold([chunk(n)])
                return n + 1, weights_alive()

            lax.while_loop(lambda state: (state[0] <= i) & state[1], earlier,
                           (jnp.int32(2), weights_alive()))
        for hd in heads:
            o_ref[pl.ds(row0, Q_TILE), hcols[hd]] = acc_scr[hd][...].astype(BF16)

    q_tile(0, True)

    def later_tile(i, _):
        q_tile(i, False)
        return 0

    lax.fori_loop(1, S // Q_TILE, later_tile, 0)


def _attn(proj, B, S):
    T = B * S
    blk = lambda col0: pl.BlockSpec((S, HEAD_COLS), lambda b, h: (b, col0 + h))
    return pl.pallas_call(
        _attn_kernel,
        grid=(B, SB_HEADS // HEADS_PER_STEP),
        in_specs=[blk(Q_COL0), blk(K_COL0), blk(V_COL0)],
        out_specs=pl.BlockSpec((S, HEAD_COLS), lambda b, h: (b, h)),
        out_shape=jax.ShapeDtypeStruct((T, D_MODEL), BF16),
        scratch_shapes=([pltpu.VMEM((Q_TILE, KEY_BLOCK), F32)] * HEADS_PER_STEP
                        + [pltpu.VMEM((Q_TILE, SB_HEAD_DIM), F32)] * HEADS_PER_STEP),
        compiler_params=_cparams("parallel", "parallel"),
        name="attn",
    )(proj, proj, proj)


def _merge_kernel(x_ref, yp_ref, yl_ref, ys_ref, g0_ref, g1_ref, g2_ref, bg_ref, wb_ref,
                  wo_ref, o_ref):
    merged = None
    for n, (y_ref, g_ref) in enumerate(((yp_ref, g0_ref), (yl_ref, g1_ref), (ys_ref, g2_ref))):
        gate = jax.nn.sigmoid(g_ref[...].astype(F32) + bg_ref[n:n + 1, :])
        term = gate * jnp.dot(y_ref[...], wb_ref[n], preferred_element_type=F32)
        merged = term if merged is None else merged + term
    o_ref[...] = x_ref[...] + jnp.dot(merged.astype(BF16), wo_ref[...],
                                      preferred_element_type=F32)


def _merge(x2, yp, yl, ys, proj, layer, bg, wb, wo, tm):
    T = x2.shape[0]
    tile = pl.BlockSpec((tm, D_MODEL), lambda i: (i, 0))
    gate = lambda n: pl.BlockSpec((tm, D_MODEL), lambda i: (i, GATE_COL0 + n))
    return pl.pallas_call(
        _merge_kernel,
        grid=(T // tm,),
        in_specs=[tile, tile, tile, tile, gate(0), gate(1), gate(2),
                  _const_spec((3, D_MODEL)),
                  _const_spec((3, D_MODEL, D_MODEL), layer),
                  _const_spec((D_MODEL, D_MODEL), layer)],
        out_specs=tile,
        out_shape=jax.ShapeDtypeStruct((T, D_MODEL), F32),
        compiler_params=_cparams("parallel"),
        name="merge",
    )(x2, yp, yl, ys, proj, proj, proj, bg, wb, wo)


def _ffn_kernel(x_ref, g_ref, wg_ref, wu_ref, wd_ref, o_ref):
    x = x_ref[...]
    h = _rms(x, g_ref[...]).astype(BF16)
    gate = jnp.dot(h, wg_ref[...], preferred_element_type=F32)
    up = jnp.dot(h, wu_ref[...], preferred_element_type=F32)
    act = (gate * jax.nn.sigmoid(gate) * up).astype(BF16)
    o_ref[...] = x + jnp.dot(act, wd_ref[...], preferred_element_type=F32)


def _ffn(x2, layer, g, wg, wu, wd, tm):
    T = x2.shape[0]
    tile = pl.BlockSpec((tm, D_MODEL), lambda i: (i, 0))
    return pl.pallas_call(
        _ffn_kernel,
        grid=(T // tm,),
        in_specs=[tile, _const_spec((1, D_MODEL)), _const_spec((D_MODEL, D_FF), layer),
                  _const_spec((D_MODEL, D_FF), layer), _const_spec((D_FF, D_MODEL), layer)],
        out_specs=tile,
        out_shape=jax.ShapeDtypeStruct((T, D_MODEL), F32),
        compiler_params=_cparams("parallel"),
        name="ffn",
    )(x2, g, wg, wu, wd)


def _gate_tiles(w_rg, w_ig):
    depth = w_rg.shape[0]
    nt = D_MODEL // CH_TILE
    per = CH_TILE // LRU_BLOCK
    eye = jnp.eye(per, dtype=F32)

    def tiles(w):
        w5 = w.reshape(depth, nt, per, LRU_BLOCK, LRU_BLOCK)
        return jnp.einsum('ltbij,bc->ltbicj', w5, eye).reshape(depth, nt, CH_TILE, CH_TILE)

    return jnp.concatenate([tiles(w_rg), tiles(w_ig)], axis=-1).astype(BF16)


def kernel(x, norm_mix, w_in, b_gate, w_pool, pool_scale, conv_w, conv_b, w_rg, b_rg, w_ig, b_ig,
           lru_lambda, q_norm, k_norm, w_branch, w_out, norm_ffn, w_ffn_gate, w_ffn_up,
           w_ffn_down):
    B, S, D = x.shape
    assert D == D_MODEL and S % Q_TILE == 0
    T = B * S
    tm = min(512, S)
    assert S % tm == 0
    depth = w_in.shape[0]
    row = lambda v: v.reshape(1, -1)

    w_in, w_pool, w_branch, w_out, w_ffn_gate, w_ffn_up, w_ffn_down = (
        w.astype(BF16) for w in (w_in, w_pool, w_branch, w_out, w_ffn_gate, w_ffn_up, w_ffn_down))
    w_gate = _gate_tiles(w_rg, w_ig)

    x2 = x.reshape(T, D)
    for l in range(depth):
        q_gain = q_norm[l] * (SB_HEAD_DIM ** -0.5 * LOG2E)
        qk_gain = jnp.stack([jnp.tile(q_gain, SB_HEADS), jnp.tile(k_norm[l], SB_HEADS)])
        proj, yp, yl = _front(x2, B, S, l, row(norm_mix[l]), w_in, qk_gain, w_pool,
                              row(pool_scale[l]), conv_w[l], row(conv_b[l]), w_gate,
                              row(b_rg[l]), row(b_ig[l]), row(lru_lambda[l]), tm)
        ys = _attn(proj, B, S)
        x2 = _merge(x2, yp, yl, ys, proj, l, b_gate[l].reshape(3, D), w_branch, w_out, tm)
        x2 = _ffn(x2, l, row(norm_ffn[l]), w_ffn_gate, w_ffn_up, w_ffn_down, tm)
    return x2.reshape(B, S, D)
```

```python
import functools

import jax
import jax.numpy as jnp
from jax import lax
from jax.experimental import pallas as pl
from jax.experimental.pallas import tpu as pltpu

F32 = jnp.float32
BF16 = jnp.bfloat16

D_MODEL = 1024
EPS = 1e-6
N_POOL_GROUPS = 4
POOL_GROUP = 256
LRU_BLOCKS = 16
LRU_BLOCK = 64
CONV_WIDTH = 4
LRU_C = 8.0
SB_HEADS = 8
SB_HEAD_DIM = 128
D_FF = 2816
IN_COLS = 8 * D_MODEL

POOL_CHUNK, LRU_CHUNK, Q_CHUNK, K_CHUNK, V_CHUNK = range(5)
PROJ_COLS = IN_COLS - Q_CHUNK * D_MODEL
CH_TILE = 256
POOL_HALO = 16
HEADS_PER_STEP = 4
HEAD_COLS = HEADS_PER_STEP * SB_HEAD_DIM
Q_COL0 = (Q_CHUNK - Q_CHUNK) * D_MODEL // HEAD_COLS
K_COL0 = (K_CHUNK - Q_CHUNK) * D_MODEL // HEAD_COLS
V_COL0 = (V_CHUNK - Q_CHUNK) * D_MODEL // HEAD_COLS
GATE_COL0 = V_CHUNK - Q_CHUNK + 1
LOG2E = 1.4426950408889634

KEY_BLOCK = 128
Q_TILE = 256
UNDERFLOW_BITS = 150.0
ROW_TILE = 8
VMEM_LIMIT = 56 * 1024 * 1024


def _cparams(*sem):
    return pltpu.CompilerParams(dimension_semantics=sem, vmem_limit_bytes=VMEM_LIMIT)


def _const_spec(shape, layer=None):
    zeros = (0,) * len(shape)
    if layer is None:
        return pl.BlockSpec(shape, lambda *_: zeros, pipeline_mode=pl.Buffered(1))
    return pl.BlockSpec((None,) + tuple(shape), lambda *_: (layer,) + zeros,
                        pipeline_mode=pl.Buffered(1))


def _rms(x, g):
    return x * lax.rsqrt(jnp.mean(x * x, axis=-1, keepdims=True) + EPS) * g


def _front_kernel(x_ref, g_ref, w_ref, qkg_ref, wpool_ref, pscale_ref, convw_ref, convb_ref,
                  wgate_ref, brg_ref, big_ref, lam_ref, proj_ref, yp_ref, yl_ref,
                  pool_tail, lru_tail, h_state, a_scr, b_scr):
    TM = x_ref.shape[0]
    t = pl.program_id(1)

    @pl.when(t == 0)
    def _():
        pool_tail[...] = jnp.zeros_like(pool_tail)
        lru_tail[...] = jnp.zeros_like(lru_tail)
        h_state[...] = jnp.zeros_like(h_state)

    h = _rms(x_ref[...], g_ref[...]).astype(BF16)

    def project(c):
        return jnp.dot(h, w_ref[:, c * D_MODEL:(c + 1) * D_MODEL], preferred_element_type=F32)

    def head_norm(r, c):
        heads = []
        for hd in range(SB_HEADS):
            seg = r[:, hd * SB_HEAD_DIM:(hd + 1) * SB_HEAD_DIM]
            ms = jnp.mean(seg * seg, axis=-1, keepdims=True)
            heads.append(seg * lax.rsqrt(ms + EPS))
        return jnp.concatenate(heads, axis=1) * qkg_ref[c - Q_CHUNK:c - Q_CHUNK + 1, :]

    def emit(c):
        r = project(c)
        if c in (Q_CHUNK, K_CHUNK):
            r = head_norm(r, c)
        proj_ref[:, (c - Q_CHUNK) * D_MODEL:(c - Q_CHUNK + 1) * D_MODEL] = r.astype(BF16)

    u_pool = project(POOL_CHUNK)
    u_lru = project(LRU_CHUNK)

    pos = t * TM + 1 + lax.broadcasted_iota(jnp.int32, (POOL_HALO, CH_TILE), 0)
    lam = lam_ref[...]
    softplus_neg_lam = jnp.maximum(-lam, 0.0) + jnp.log1p(jnp.exp(-jnp.abs(lam)))

    sub = lax.broadcasted_iota(jnp.int32, (TM // ROW_TILE, ROW_TILE, CH_TILE), 1)
    later_chunks = list(range(Q_CHUNK, IN_COLS // D_MODEL))
    for j in range(D_MODEL // CH_TILE):
        cols = slice(j * CH_TILE, (j + 1) * CH_TILE)

        window = 2 << j
        u = u_pool[:, cols]
        s = jnp.concatenate([pool_tail[:, cols], u], axis=0)
        pool_tail[:, cols] = u[TM - POOL_HALO:]
        for d in (1, 2, 4, 8)[:j + 1]:
            s = s + pltpu.roll(s, d, axis=0)
        s = s[POOL_HALO:]
        head_inv = 1.0 / jnp.minimum(pos, window).astype(F32)
        mean = jnp.concatenate([s[:POOL_HALO] * head_inv, s[POOL_HALO:] * (1.0 / window)], axis=0)
        y = jnp.dot((mean - u).astype(BF16), wpool_ref[j], preferred_element_type=F32)
        yp_ref[:, cols] = (y * pscale_ref[:, cols]).astype(BF16)

        ul = u_lru[:, cols]
        hist = jnp.concatenate([lru_tail[:, cols], ul], axis=0)
        lru_tail[:, cols] = ul[TM - ROW_TILE:]
        xc = convb_ref[:, cols] + convw_ref[CONV_WIDTH - 1:CONV_WIDTH, cols] * ul
        for k in range(CONV_WIDTH - 1):
            shifted = pltpu.roll(hist, CONV_WIDTH - 1 - k, axis=0)[ROW_TILE:]
            xc = xc + convw_ref[k:k + 1, cols] * shifted
        gates = jnp.dot(xc.astype(BF16), wgate_ref[j], preferred_element_type=F32)
        decay = (-0.5 * LRU_C) * softplus_neg_lam[:, cols]
        log_a = decay + decay * jnp.tanh(0.5 * (gates[:, :CH_TILE] + brg_ref[:, cols]))
        ig = 0.5 + 0.5 * jnp.tanh(0.5 * (gates[:, CH_TILE:] + big_ref[:, cols]))
        a = jnp.exp(log_a)
        b = jnp.sqrt(-jnp.tanh(log_a) * (a * a + 1.0)) * (ig * xc)

        a = a.reshape(TM // ROW_TILE, ROW_TILE, CH_TILE)
        b = b.reshape(TM // ROW_TILE, ROW_TILE, CH_TILE)
        for d in (1, 2, 4):
            keep = sub >= d
            b = a * jnp.where(keep, pltpu.roll(b, d, axis=1), 0.0) + b
            a = a * jnp.where(keep, pltpu.roll(a, d, axis=1), 1.0)
        a_scr[:, :, cols] = a
        b_scr[:, :, cols] = b

        n_emit = -(-len(later_chunks) // (D_MODEL // CH_TILE - j))
        for c in later_chunks[:n_emit]:
            emit(c)
        later_chunks = later_chunks[n_emit:]

    def scan_tile(n, h_prev):
        hh = a_scr[n] * h_prev + b_scr[n]
        yl_ref[pl.ds(pl.multiple_of(n * ROW_TILE, ROW_TILE), ROW_TILE), :] = hh.astype(BF16)
        return jnp.broadcast_to(hh[ROW_TILE - 1:ROW_TILE, :], (ROW_TILE, D_MODEL))

    h_state[...] = lax.fori_loop(0, TM // ROW_TILE, scan_tile, h_state[...], unroll=4)


def _front(x2, B, S, layer, g, w, qk_gain, wpool, pscale, convw, convb, wgate, brg, big, lam, tm):
    T = B * S
    nt = S // tm
    tile = lambda width: pl.BlockSpec((tm, width), lambda b, t: (b * nt + t, 0))
    vec = _const_spec((1, D_MODEL))
    return pl.pallas_call(
        _front_kernel,
        grid=(B, nt),
        in_specs=[tile(D_MODEL), vec,
                  _const_spec((D_MODEL, IN_COLS), layer),
                  _const_spec((2, D_MODEL)),
                  _const_spec((N_POOL_GROUPS, POOL_GROUP, POOL_GROUP), layer),
                  vec,
                  _const_spec((CONV_WIDTH, D_MODEL)),
                  vec,
                  _const_spec((D_MODEL // CH_TILE, CH_TILE, 2 * CH_TILE), layer),
                  vec, vec, vec],
        out_specs=[tile(PROJ_COLS), tile(D_MODEL), tile(D_MODEL)],
        out_shape=[jax.ShapeDtypeStruct((T, PROJ_COLS), BF16),
                   jax.ShapeDtypeStruct((T, D_MODEL), BF16),
                   jax.ShapeDtypeStruct((T, D_MODEL), BF16)],
        scratch_shapes=[pltpu.VMEM((POOL_HALO, D_MODEL), F32), pltpu.VMEM((ROW_TILE, D_MODEL), F32),
                        pltpu.VMEM((ROW_TILE, D_MODEL), F32),
                        pltpu.VMEM((tm // ROW_TILE, ROW_TILE, D_MODEL), F32),
                        pltpu.VMEM((tm // ROW_TILE, ROW_TILE, D_MODEL), F32)],
        compiler_params=_cparams("parallel", "arbitrary"),
        name="front",
    )(x2, g, w, qk_gain, wpool, pscale, convw, convb, wgate, brg, big, lam)


def _attn_kernel(q_ref, k_ref, v_ref, o_ref, *scratch):
    S = q_ref.shape[0]
    carry_scr, acc_scr = scratch[:HEADS_PER_STEP], scratch[HEADS_PER_STEP:]

    rr = lax.broadcasted_iota(jnp.int32, (2 * KEY_BLOCK, 2 * KEY_BLOCK), 0) % KEY_BLOCK
    cc = lax.broadcasted_iota(jnp.int32, (2 * KEY_BLOCK, 2 * KEY_BLOCK), 1)
    suffix = jnp.where((cc >= KEY_BLOCK) | (rr > cc), 1.0, 0.0).astype(BF16)
    heads = range(HEADS_PER_STEP)
    hcols = [slice(hd * SB_HEAD_DIM, (hd + 1) * SB_HEAD_DIM) for hd in heads]

    def fold(jobs):
        def visible(job):
            _, _, n_rows, _, n_keys, lead = job
            if lead is None:
                return None
            r = lax.broadcasted_iota(jnp.int32, (n_rows, n_keys), 0)
            c = lax.broadcasted_iota(jnp.int32, (n_rows, n_keys), 1)
            return c < r + lead

        work = [(job, visible(job), hd) for job in jobs for hd in heads]
        z = [lax.dot_general(q_ref[pl.ds(row0 + sub0, n_rows), hcols[hd]],
                             k_ref[pl.ds(pl.multiple_of(key0, Q_TILE), n_keys), hcols[hd]],
                             (((1,), (1,)), ((), ())), preferred_element_type=F32)
             for (row0, sub0, n_rows, key0, n_keys, _), _, hd in work]
        log_beta, sums = [], []
        for (job, mask, hd), zz in zip(work, z):
            lo_z = jnp.minimum(zz, 0.0)
            hi_z = jnp.maximum(zz, 0.0)
            softplus = jnp.log(1.0 + jnp.exp2(lo_z - hi_z)) * LOG2E
            log_beta.append(lo_z - softplus)
            drop = hi_z + softplus
            if mask is not None:
                drop = jnp.where(mask, drop, 0.0)
            hi = drop.astype(BF16)
            lo = (drop - hi.astype(F32)).astype(BF16)
            sums.append([jnp.dot(jnp.concatenate([hi[:, c * KEY_BLOCK:(c + 1) * KEY_BLOCK],
                                                  lo[:, c * KEY_BLOCK:(c + 1) * KEY_BLOCK]], axis=1),
                                 suffix, preferred_element_type=F32)
                         for c in range(job[4] // KEY_BLOCK)])
        for n, (job, mask, hd) in enumerate(work):
            _, sub0, n_rows, key0, n_keys, _ = job
            rows = slice(sub0, sub0 + n_rows)
            n_blk = n_keys // KEY_BLOCK
            carry = carry_scr[hd][rows, :]
            ws = [None] * n_blk
            for c in reversed(range(n_blk)):
                cols = slice(c * KEY_BLOCK, (c + 1) * KEY_BLOCK)
                w = jnp.exp2(log_beta[n][:, cols] - sums[n][c][:, :KEY_BLOCK] - carry)
                if mask is not None:
                    w = jnp.where(mask[:, cols], w, 0.0)
                ws[c] = w.astype(BF16)
                carry = carry + sums[n][c][:, KEY_BLOCK:]
            carry_scr[hd][rows, :] = carry
            acc_scr[hd][rows, :] += jnp.dot(
                jnp.concatenate(ws, axis=1),
                v_ref[pl.ds(pl.multiple_of(key0, Q_TILE), n_keys), hcols[hd]],
                preferred_element_type=F32)

    def q_tile(i, first):
        row0 = pl.multiple_of(i * Q_TILE, Q_TILE)
        for ref in scratch:
            ref[...] = jnp.zeros_like(ref)

        def chunk(n):
            return (row0, 0, Q_TILE, row0 - n * Q_TILE, Q_TILE, None)

        half = Q_TILE // 2
        diagonal = [(row0, 0, half, row0, half, 0), (row0, half, half, row0, Q_TILE, half)]
        if first:
            fold(diagonal)
        else:
            fold(diagonal + [chunk(1)])

            def weights_alive():
                least = functools.reduce(jnp.minimum, [ref[...] for ref in carry_scr])
                return jnp.min(least) < UNDERFLOW_BITS

            def earlier(state):
                n, _ = state
                fold([chunk(n)])
                return n + 1, weights_alive()

            lax.while_loop(lambda state: (state[0] <= i) & state[1], earlier,
                           (jnp.int32(2), weights_alive()))
        for hd in heads:
            o_ref[pl.ds(row0, Q_TILE), hcols[hd]] = acc_scr[hd][...].astype(BF16)

    q_tile(0, True)

    def later_tile(i, _):
        q_tile(i, False)
        return 0

    lax.fori_loop(1, S // Q_TILE, later_tile, 0)


def _attn(proj, B, S):
    T = B * S
    blk = lambda col0: pl.BlockSpec((S, HEAD_COLS), lambda b, h: (b, col0 + h))
    return pl.pallas_call(
        _attn_kernel,
        grid=(B, SB_HEADS // HEADS_PER_STEP),
        in_specs=[blk(Q_COL0), blk(K_COL0), blk(V_COL0)],
        out_specs=pl.BlockSpec((S, HEAD_COLS), lambda b, h: (b, h)),
        out_shape=jax.ShapeDtypeStruct((T, D_MODEL), BF16),
        scratch_shapes=([pltpu.VMEM((Q_TILE, KEY_BLOCK), F32)] * HEADS_PER_STEP
                        + [pltpu.VMEM((Q_TILE, SB_HEAD_DIM), F32)] * HEADS_PER_STEP),
        compiler_params=_cparams("parallel", "parallel"),
        name="attn",
    )(proj, proj, proj)


def _back_kernel(x_ref, yp_ref, yl_ref, ys_ref, g0_ref, g1_ref, g2_ref, bg_ref, wb_ref, wo_ref,
                 nf_ref, wg_ref, wu_ref, wd_ref, o_ref):
    merged = None
    for n, (y_ref, g_ref) in enumerate(((yp_ref, g0_ref), (yl_ref, g1_ref), (ys_ref, g2_ref))):
        gate = jax.nn.sigmoid(g_ref[...].astype(F32) + bg_ref[n:n + 1, :])
        term = gate * jnp.dot(y_ref[...], wb_ref[n], preferred_element_type=F32)
        merged = term if merged is None else merged + term
    x = x_ref[...] + jnp.dot(merged.astype(BF16), wo_ref[...], preferred_element_type=F32)
    h = _rms(x, nf_ref[...]).astype(BF16)
    gate = jnp.dot(h, wg_ref[...], preferred_element_type=F32)
    up = jnp.dot(h, wu_ref[...], preferred_element_type=F32)
    act = (gate * jax.nn.sigmoid(gate) * up).astype(BF16)
    o_ref[...] = x + jnp.dot(act, wd_ref[...], preferred_element_type=F32)


def _back(x2, yp, yl, ys, proj, layer, bg, wb, wo, nf, wg, wu, wd, tm):
    T = x2.shape[0]
    tile = pl.BlockSpec((tm, D_MODEL), lambda i: (i, 0))
    gate = lambda n: pl.BlockSpec((tm, D_MODEL), lambda i: (i, GATE_COL0 + n))
    return pl.pallas_call(
        _back_kernel,
        grid=(T // tm,),
        in_specs=[tile, tile, tile, tile, gate(0), gate(1), gate(2),
                  _const_spec((3, D_MODEL)),
                  _const_spec((3, D_MODEL, D_MODEL), layer),
                  _const_spec((D_MODEL, D_MODEL), layer),
                  _const_spec((1, D_MODEL)), _const_spec((D_MODEL, D_FF), layer),
                  _const_spec((D_MODEL, D_FF), layer), _const_spec((D_FF, D_MODEL), layer)],
        out_specs=tile,
        out_shape=jax.ShapeDtypeStruct((T, D_MODEL), F32),
        compiler_params=_cparams("parallel"),
        name="back",
    )(x2, yp, yl, ys, proj, proj, proj, bg, wb, wo, nf, wg, wu, wd)


def _gate_tiles(w_rg, w_ig):
    depth = w_rg.shape[0]
    nt = D_MODEL // CH_TILE
    per = CH_TILE // LRU_BLOCK
    eye = jnp.eye(per, dtype=F32)

    def tiles(w):
        w5 = w.reshape(depth, nt, per, LRU_BLOCK, LRU_BLOCK)
        return jnp.einsum('ltbij,bc->ltbicj', w5, eye).reshape(depth, nt, CH_TILE, CH_TILE)

    return jnp.concatenate([tiles(w_rg), tiles(w_ig)], axis=-1).astype(BF16)


def kernel(x, norm_mix, w_in, b_gate, w_pool, pool_scale, conv_w, conv_b, w_rg, b_rg, w_ig, b_ig,
           lru_lambda, q_norm, k_norm, w_branch, w_out, norm_ffn, w_ffn_gate, w_ffn_up,
           w_ffn_down):
    B, S, D = x.shape
    assert D == D_MODEL and S % Q_TILE == 0
    T = B * S
    tm = min(512, S)
    assert S % tm == 0
    depth = w_in.shape[0]
    row = lambda v: v.reshape(1, -1)

    w_in, w_pool, w_branch, w_out, w_ffn_gate, w_ffn_up, w_ffn_down = (
        w.astype(BF16) for w in (w_in, w_pool, w_branch, w_out, w_ffn_gate, w_ffn_up, w_ffn_down))
    w_gate = _gate_tiles(w_rg, w_ig)

    x2 = x.reshape(T, D)
    for l in range(depth):
        q_gain = q_norm[l] * (SB_HEAD_DIM ** -0.5 * LOG2E)
        qk_gain = jnp.stack([jnp.tile(q_gain, SB_HEADS), jnp.tile(k_norm[l], SB_HEADS)])
        proj, yp, yl = _front(x2, B, S, l, row(norm_mix[l]), w_in, qk_gain, w_pool,
                              row(pool_scale[l]), conv_w[l], row(conv_b[l]), w_gate,
                              row(b_rg[l]), row(b_ig[l]), row(lru_lambda[l]), tm)
        ys = _attn(proj, B, S)
        x2 = _back(x2, yp, yl, ys, proj, l, b_gate[l].reshape(3, D), w_branch, w_out,
                   row(norm_ffn[l]), w_ffn_gate, w_ffn_up, w_ffn_down, tm)
    return x2.reshape(B, S, D)
```

```python
import functools

import jax
import jax.numpy as jnp
from jax import lax
from jax.experimental import pallas as pl
from jax.experimental.pallas import tpu as pltpu

F32 = jnp.float32
BF16 = jnp.bfloat16

D_MODEL = 1024
EPS = 1e-6
N_POOL_GROUPS = 4
POOL_GROUP = 256
LRU_BLOCKS = 16
LRU_BLOCK = 64
CONV_WIDTH = 4
LRU_C = 8.0
SB_HEADS = 8
SB_HEAD_DIM = 128
D_FF = 2816
IN_COLS = 8 * D_MODEL

POOL_CHUNK, LRU_CHUNK, Q_CHUNK, K_CHUNK, V_CHUNK = range(5)
PROJ_COLS = IN_COLS - Q_CHUNK * D_MODEL
CH_TILE = 256
POOL_HALO = 16
HEADS_PER_STEP = 4
HEAD_COLS = HEADS_PER_STEP * SB_HEAD_DIM
Q_COL0 = (Q_CHUNK - Q_CHUNK) * D_MODEL // HEAD_COLS
K_COL0 = (K_CHUNK - Q_CHUNK) * D_MODEL // HEAD_COLS
V_COL0 = (V_CHUNK - Q_CHUNK) * D_MODEL // HEAD_COLS
GATE_COL0 = V_CHUNK - Q_CHUNK + 1
LOG2E = 1.4426950408889634

KEY_BLOCK = 128
Q_TILE = 256
MASKED_SCORE = -3.0e38
UNDERFLOW_BITS = 150.0
ROW_TILE = 8
VMEM_LIMIT = 56 * 1024 * 1024


def _cparams(*sem):
    return pltpu.CompilerParams(dimension_semantics=sem, vmem_limit_bytes=VMEM_LIMIT)


def _const_spec(shape, layer=None):
    zeros = (0,) * len(shape)
    if layer is None:
        return pl.BlockSpec(shape, lambda *_: zeros, pipeline_mode=pl.Buffered(1))
    return pl.BlockSpec((None,) + tuple(shape), lambda *_: (layer,) + zeros,
                        pipeline_mode=pl.Buffered(1))


def _rms(x, g):
    return x * lax.rsqrt(jnp.mean(x * x, axis=-1, keepdims=True) + EPS) * g


def _front_kernel(x_ref, g_ref, w_ref, qkg_ref, wpool_ref, pscale_ref, convw_ref, convb_ref,
                  wgate_ref, brg_ref, big_ref, lam_ref, proj_ref, yp_ref, yl_ref,
                  pool_tail, lru_tail, h_state, a_scr, b_scr):
    TM = x_ref.shape[0]
    t = pl.program_id(1)

    @pl.when(t == 0)
    def _():
        pool_tail[...] = jnp.zeros_like(pool_tail)
        lru_tail[...] = jnp.zeros_like(lru_tail)
        h_state[...] = jnp.zeros_like(h_state)

    h = _rms(x_ref[...], g_ref[...]).astype(BF16)

    def project(c):
        return jnp.dot(h, w_ref[:, c * D_MODEL:(c + 1) * D_MODEL], preferred_element_type=F32)

    def head_norm(r, c):
        heads = []
        for hd in range(SB_HEADS):
            seg = r[:, hd * SB_HEAD_DIM:(hd + 1) * SB_HEAD_DIM]
            ms = jnp.mean(seg * seg, axis=-1, keepdims=True)
            heads.append(seg * lax.rsqrt(ms + EPS))
        return jnp.concatenate(heads, axis=1) * qkg_ref[c - Q_CHUNK:c - Q_CHUNK + 1, :]

    def emit(c):
        r = project(c)
        if c in (Q_CHUNK, K_CHUNK):
            r = head_norm(r, c)
        proj_ref[:, (c - Q_CHUNK) * D_MODEL:(c - Q_CHUNK + 1) * D_MODEL] = r.astype(BF16)

    u_pool = project(POOL_CHUNK)
    u_lru = project(LRU_CHUNK)

    pos = t * TM + 1 + lax.broadcasted_iota(jnp.int32, (POOL_HALO, CH_TILE), 0)
    lam = lam_ref[...]
    softplus_neg_lam = jnp.maximum(-lam, 0.0) + jnp.log1p(jnp.exp(-jnp.abs(lam)))

    sub = lax.broadcasted_iota(jnp.int32, (TM // ROW_TILE, ROW_TILE, CH_TILE), 1)
    later_chunks = list(range(Q_CHUNK, IN_COLS // D_MODEL))
    for j in range(D_MODEL // CH_TILE):
        cols = slice(j * CH_TILE, (j + 1) * CH_TILE)

        window = 2 << j
        u = u_pool[:, cols]
        s = jnp.concatenate([pool_tail[:, cols], u], axis=0)
        pool_tail[:, cols] = u[TM - POOL_HALO:]
        for d in (1, 2, 4, 8)[:j + 1]:
            s = s + pltpu.roll(s, d, axis=0)
        s = s[POOL_HALO:]
        head_inv = 1.0 / jnp.minimum(pos, window).astype(F32)
        mean = jnp.concatenate([s[:POOL_HALO] * head_inv, s[POOL_HALO:] * (1.0 / window)], axis=0)
        y = jnp.dot((mean - u).astype(BF16), wpool_ref[j], preferred_element_type=F32)
        yp_ref[:, cols] = (y * pscale_ref[:, cols]).astype(BF16)

        ul = u_lru[:, cols]
        hist = jnp.concatenate([lru_tail[:, cols], ul], axis=0)
        lru_tail[:, cols] = ul[TM - ROW_TILE:]
        xc = convb_ref[:, cols] + convw_ref[CONV_WIDTH - 1:CONV_WIDTH, cols] * ul
        for k in range(CONV_WIDTH - 1):
            shifted = pltpu.roll(hist, CONV_WIDTH - 1 - k, axis=0)[ROW_TILE:]
            xc = xc + convw_ref[k:k + 1, cols] * shifted
        gates = jnp.dot(xc.astype(BF16), wgate_ref[j], preferred_element_type=F32)
        decay = (-0.5 * LRU_C) * softplus_neg_lam[:, cols]
        log_a = decay + decay * jnp.tanh(gates[:, :CH_TILE] + brg_ref[:, cols])
        ig = 0.5 + 0.5 * jnp.tanh(gates[:, CH_TILE:] + big_ref[:, cols])
        a = jnp.exp(log_a)
        b = jnp.sqrt(-jnp.tanh(log_a) * (a * a + 1.0)) * (ig * xc)

        a = a.reshape(TM // ROW_TILE, ROW_TILE, CH_TILE)
        b = b.reshape(TM // ROW_TILE, ROW_TILE, CH_TILE)
        for d in (1, 2, 4):
            keep = sub >= d
            b = a * jnp.where(keep, pltpu.roll(b, d, axis=1), 0.0) + b
            a = a * jnp.where(keep, pltpu.roll(a, d, axis=1), 1.0)
        a_scr[:, :, cols] = a
        b_scr[:, :, cols] = b

        n_emit = -(-len(later_chunks) // (D_MODEL // CH_TILE - j))
        for c in later_chunks[:n_emit]:
            emit(c)
        later_chunks = later_chunks[n_emit:]

    def scan_tile(n, h_prev):
        hh = a_scr[n] * h_prev + b_scr[n]
        yl_ref[pl.ds(pl.multiple_of(n * ROW_TILE, ROW_TILE), ROW_TILE), :] = hh.astype(BF16)
        return jnp.broadcast_to(hh[ROW_TILE - 1:ROW_TILE, :], (ROW_TILE, D_MODEL))

    h_state[...] = lax.fori_loop(0, TM // ROW_TILE, scan_tile, h_state[...], unroll=4)


def _front(x2, B, S, layer, g, w, qk_gain, wpool, pscale, convw, convb, wgate, brg, big, lam, tm):
    T = B * S
    nt = S // tm
    tile = lambda width: pl.BlockSpec((tm, width), lambda b, t: (b * nt + t, 0))
    vec = _const_spec((1, D_MODEL))
    return pl.pallas_call(
        _front_kernel,
        grid=(B, nt),
        in_specs=[tile(D_MODEL), vec,
                  _const_spec((D_MODEL, IN_COLS), layer),
                  _const_spec((2, D_MODEL)),
                  _const_spec((N_POOL_GROUPS, POOL_GROUP, POOL_GROUP), layer),
                  vec,
                  _const_spec((CONV_WIDTH, D_MODEL)),
                  vec,
                  _const_spec((D_MODEL // CH_TILE, CH_TILE, 2 * CH_TILE), layer),
                  vec, vec, vec],
        out_specs=[tile(PROJ_COLS), tile(D_MODEL), tile(D_MODEL)],
        out_shape=[jax.ShapeDtypeStruct((T, PROJ_COLS), BF16),
                   jax.ShapeDtypeStruct((T, D_MODEL), BF16),
                   jax.ShapeDtypeStruct((T, D_MODEL), BF16)],
        scratch_shapes=[pltpu.VMEM((POOL_HALO, D_MODEL), F32), pltpu.VMEM((ROW_TILE, D_MODEL), F32),
                        pltpu.VMEM((ROW_TILE, D_MODEL), F32),
                        pltpu.VMEM((tm // ROW_TILE, ROW_TILE, D_MODEL), F32),
                        pltpu.VMEM((tm // ROW_TILE, ROW_TILE, D_MODEL), F32)],
        compiler_params=_cparams("parallel", "arbitrary"),
        name="front",
    )(x2, g, w, qk_gain, wpool, pscale, convw, convb, wgate, brg, big, lam)


def _attn_kernel(q_ref, k_ref, v_ref, o_ref, *scratch):
    S = q_ref.shape[0]
    carry_scr, acc_scr = scratch[:HEADS_PER_STEP], scratch[HEADS_PER_STEP:]

    rr = lax.broadcasted_iota(jnp.int32, (2 * KEY_BLOCK, 2 * KEY_BLOCK), 0) % KEY_BLOCK
    cc = lax.broadcasted_iota(jnp.int32, (2 * KEY_BLOCK, 2 * KEY_BLOCK), 1)
    suffix = jnp.where((cc >= KEY_BLOCK) | (rr > cc), 1.0, 0.0).astype(BF16)
    heads = range(HEADS_PER_STEP)
    hcols = [slice(hd * SB_HEAD_DIM, (hd + 1) * SB_HEAD_DIM) for hd in heads]

    def fold(jobs):
        def visible(job):
            _, _, n_rows, _, n_keys, lead = job
            if lead is None:
                return None
            r = lax.broadcasted_iota(jnp.int32, (n_rows, n_keys), 0)
            c = lax.broadcasted_iota(jnp.int32, (n_rows, n_keys), 1)
            return c < r + lead

        work = [(job, visible(job), hd) for job in jobs for hd in heads]
        z = [lax.dot_general(q_ref[pl.ds(row0 + sub0, n_rows), hcols[hd]],
                             k_ref[pl.ds(pl.multiple_of(key0, Q_TILE), n_keys), hcols[hd]],
                             (((1,), (1,)), ((), ())), preferred_element_type=F32)
             for (row0, sub0, n_rows, key0, n_keys, _), _, hd in work]
        log_beta, sums = [], []
        for (job, mask, hd), zz in zip(work, z):
            if mask is not None:
                zz = jnp.where(mask, zz, MASKED_SCORE)
            lo_z = jnp.minimum(zz, 0.0)
            hi_z = jnp.maximum(zz, 0.0)
            softplus = jnp.log(1.0 + jnp.exp2(lo_z - hi_z)) * LOG2E
            log_beta.append(lo_z - softplus)
            drop = hi_z + softplus
            hi = drop.astype(BF16)
            lo = (drop - hi.astype(F32)).astype(BF16)
            sums.append([jnp.dot(jnp.concatenate([hi[:, c * KEY_BLOCK:(c + 1) * KEY_BLOCK],
                                                  lo[:, c * KEY_BLOCK:(c + 1) * KEY_BLOCK]], axis=1),
                                 suffix, preferred_element_type=F32)
                         for c in range(job[4] // KEY_BLOCK)])
        for n, (job, mask, hd) in enumerate(work):
            _, sub0, n_rows, key0, n_keys, _ = job
            rows = slice(sub0, sub0 + n_rows)
            n_blk = n_keys // KEY_BLOCK
            carry = carry_scr[hd][rows, :]
            ws = [None] * n_blk
            for c in reversed(range(n_blk)):
                cols = slice(c * KEY_BLOCK, (c + 1) * KEY_BLOCK)
                w = jnp.exp2(log_beta[n][:, cols] - sums[n][c][:, :KEY_BLOCK] - carry)
                ws[c] = w.astype(BF16)
                carry = carry + sums[n][c][:, KEY_BLOCK:]
            carry_scr[hd][rows, :] = carry
            acc_scr[hd][rows, :] += jnp.dot(
                jnp.concatenate(ws, axis=1),
                v_ref[pl.ds(pl.multiple_of(key0, Q_TILE), n_keys), hcols[hd]],
                preferred_element_type=F32)

    def q_tile(i, first):
        row0 = pl.multiple_of(i * Q_TILE, Q_TILE)
        for ref in scratch:
            ref[...] = jnp.zeros_like(ref)

        def chunk(n):
            return (row0, 0, Q_TILE, row0 - n * Q_TILE, Q_TILE, None)

        half = Q_TILE // 2
        diagonal = [(row0, 0, half, row0, half, 0), (row0, half, half, row0, Q_TILE, half)]
        if first:
            fold(diagonal)
        else:
            fold(diagonal + [chunk(1)])

            def weights_alive():
                least = functools.reduce(jnp.minimum, [ref[...] for ref in carry_scr])
                return jnp.min(least) < UNDERFLOW_BITS

            def earlier(state):
                n, _ = state
                fold([chunk(n)])
                return n + 1, weights_alive()

            lax.while_loop(lambda state: (state[0] <= i) & state[1], earlier,
                           (jnp.int32(2), weights_alive()))
        for hd in heads:
            o_ref[pl.ds(row0, Q_TILE), hcols[hd]] = acc_scr[hd][...].astype(BF16)

    q_tile(0, True)

    def later_tile(i, _):
        q_tile(i, False)
        return 0

    lax.fori_loop(1, S // Q_TILE, later_tile, 0)


def _attn(proj, B, S):
    T = B * S
    blk = lambda col0: pl.BlockSpec((S, HEAD_COLS), lambda b, h: (b, col0 + h))
    return pl.pallas_call(
        _attn_kernel,
        grid=(B, SB_HEADS // HEADS_PER_STEP),
        in_specs=[blk(Q_COL0), blk(K_COL0), blk(V_COL0)],
        out_specs=pl.BlockSpec((S, HEAD_COLS), lambda b, h: (b, h)),
        out_shape=jax.ShapeDtypeStruct((T, D_MODEL), BF16),
        scratch_shapes=([pltpu.VMEM((Q_TILE, KEY_BLOCK), F32)] * HEADS_PER_STEP
                        + [pltpu.VMEM((Q_TILE, SB_HEAD_DIM), F32)] * HEADS_PER_STEP),
        compiler_params=_cparams("parallel", "parallel"),
        name="attn",
    )(proj, proj, proj)


def _merge_kernel(x_ref, yp_ref, yl_ref, ys_ref, g0_ref, g1_ref, g2_ref, bg_ref, wb_ref,
                  wo_ref, o_ref):
    merged = None
    for n, (y_ref, g_ref) in enumerate(((yp_ref, g0_ref), (yl_ref, g1_ref), (ys_ref, g2_ref))):
        gate = jax.nn.sigmoid(g_ref[...].astype(F32) + bg_ref[n:n + 1, :])
        term = gate * jnp.dot(y_ref[...], wb_ref[n], preferred_element_type=F32)
        merged = term if merged is None else merged + term
    o_ref[...] = x_ref[...] + jnp.dot(merged.astype(BF16), wo_ref[...],
                                      preferred_element_type=F32)


def _merge(x2, yp, yl, ys, proj, layer, bg, wb, wo, tm):
    T = x2.shape[0]
    tile = pl.BlockSpec((tm, D_MODEL), lambda i: (i, 0))
    gate = lambda n: pl.BlockSpec((tm, D_MODEL), lambda i: (i, GATE_COL0 + n))
    return pl.pallas_call(
        _merge_kernel,
        grid=(T // tm,),
        in_specs=[tile, tile, tile, tile, gate(0), gate(1), gate(2),
                  _const_spec((3, D_MODEL)),
                  _const_spec((3, D_MODEL, D_MODEL), layer),
                  _const_spec((D_MODEL, D_MODEL), layer)],
        out_specs=tile,
        out_shape=jax.ShapeDtypeStruct((T, D_MODEL), F32),
        compiler_params=_cparams("parallel"),
        name="merge",
    )(x2, yp, yl, ys, proj, proj, proj, bg, wb, wo)


def _ffn_kernel(x_ref, g_ref, wg_ref, wu_ref, wd_ref, o_ref):
    x = x_ref[...]
    h = _rms(x, g_ref[...]).astype(BF16)
    gate = jnp.dot(h, wg_ref[...], preferred_element_type=F32)
    up = jnp.dot(h, wu_ref[...], preferred_element_type=F32)
    act = (gate * jax.nn.sigmoid(gate) * up).astype(BF16)
    o_ref[...] = x + jnp.dot(act, wd_ref[...], preferred_element_type=F32)


def _ffn(x2, layer, g, wg, wu, wd, tm):
    T = x2.shape[0]
    tile = pl.BlockSpec((tm, D_MODEL), lambda i: (i, 0))
    return pl.pallas_call(
        _ffn_kernel,
        grid=(T // tm,),
        in_specs=[tile, _const_spec((1, D_MODEL)), _const_spec((D_MODEL, D_FF), layer),
                  _const_spec((D_MODEL, D_FF), layer), _const_spec((D_FF, D_MODEL), layer)],
        out_specs=tile,
        out_shape=jax.ShapeDtypeStruct((T, D_MODEL), F32),
        compiler_params=_cparams("parallel"),
        name="ffn",
    )(x2, g, wg, wu, wd)


def _gate_tiles(w_rg, w_ig):
    depth = w_rg.shape[0]
    nt = D_MODEL // CH_TILE
    per = CH_TILE // LRU_BLOCK
    eye = jnp.eye(per, dtype=F32)

    def tiles(w):
        w5 = w.reshape(depth, nt, per, LRU_BLOCK, LRU_BLOCK)
        return jnp.einsum('ltbij,bc->ltbicj', w5, eye).reshape(depth, nt, CH_TILE, CH_TILE)

    return (0.5 * jnp.concatenate([tiles(w_rg), tiles(w_ig)], axis=-1)).astype(BF16)


def kernel(x, norm_mix, w_in, b_gate, w_pool, pool_scale, conv_w, conv_b, w_rg, b_rg, w_ig, b_ig,
           lru_lambda, q_norm, k_norm, w_branch, w_out, norm_ffn, w_ffn_gate, w_ffn_up,
           w_ffn_down):
    B, S, D = x.shape
    assert D == D_MODEL and S % Q_TILE == 0
    T = B * S
    tm = min(512, S)
    assert S % tm == 0
    depth = w_in.shape[0]
    row = lambda v: v.reshape(1, -1)

    w_in, w_pool, w_branch, w_out, w_ffn_gate, w_ffn_up, w_ffn_down = (
        w.astype(BF16) for w in (w_in, w_pool, w_branch, w_out, w_ffn_gate, w_ffn_up, w_ffn_down))
    w_gate = _gate_tiles(w_rg, w_ig)

    x2 = x.reshape(T, D)
    for l in range(depth):
        q_gain = q_norm[l] * (SB_HEAD_DIM ** -0.5 * LOG2E)
        qk_gain = jnp.stack([jnp.tile(q_gain, SB_HEADS), jnp.tile(k_norm[l], SB_HEADS)])
        proj, yp, yl = _front(x2, B, S, l, row(norm_mix[l]), w_in, qk_gain, w_pool,
                              row(pool_scale[l]), conv_w[l], row(conv_b[l]), w_gate,
                              row(0.5 * b_rg[l]), row(0.5 * b_ig[l]), row(lru_lambda[l]), tm)
        ys = _attn(proj, B, S)
        x2 = _merge(x2, yp, yl, ys, proj, l, b_gate[l].reshape(3, D), w_branch, w_out, tm)
        x2 = _ffn(x2, l, row(norm_ffn[l]), w_ffn_gate, w_ffn_up, w_ffn_down, tm)
    return x2.reshape(B, S, D)
```

```python
import functools

import jax
import jax.numpy as jnp
from jax import lax
from jax.experimental import pallas as pl
from jax.experimental.pallas import tpu as pltpu

F32 = jnp.float32
BF16 = jnp.bfloat16

D_MODEL = 1024
EPS = 1e-6
N_POOL_GROUPS = 4
POOL_GROUP = 256
LRU_BLOCKS = 16
LRU_BLOCK = 64
CONV_WIDTH = 4
LRU_C = 8.0
SB_HEADS = 8
SB_HEAD_DIM = 128
D_FF = 2816
IN_COLS = 8 * D_MODEL

POOL_CHUNK, LRU_CHUNK, Q_CHUNK, K_CHUNK, V_CHUNK = range(5)
PROJ_COLS = IN_COLS - Q_CHUNK * D_MODEL
CH_TILE = 256
POOL_HALO = 16
HEADS_PER_STEP = 4
HEAD_COLS = HEADS_PER_STEP * SB_HEAD_DIM
Q_COL0 = (Q_CHUNK - Q_CHUNK) * D_MODEL // HEAD_COLS
K_COL0 = (K_CHUNK - Q_CHUNK) * D_MODEL // HEAD_COLS
V_COL0 = (V_CHUNK - Q_CHUNK) * D_MODEL // HEAD_COLS
GATE_COL0 = V_CHUNK - Q_CHUNK + 1
LOG2E = 1.4426950408889634

KEY_BLOCK = 128
Q_TILE = 256
UNDERFLOW_BITS = 150.0
ROW_TILE = 8
VMEM_LIMIT = 56 * 1024 * 1024


def _cparams(*sem):
    return pltpu.CompilerParams(dimension_semantics=sem, vmem_limit_bytes=VMEM_LIMIT)


def _const_spec(shape, layer=None):
    zeros = (0,) * len(shape)
    if layer is None:
        return pl.BlockSpec(shape, lambda *_: zeros, pipeline_mode=pl.Buffered(1))
    return pl.BlockSpec((None,) + tuple(shape), lambda *_: (layer,) + zeros,
                        pipeline_mode=pl.Buffered(1))


def _rms(x, g):
    return x * lax.rsqrt(jnp.mean(x * x, axis=-1, keepdims=True) + EPS) * g


def _front_kernel(x_ref, g_ref, w_ref, qkg_ref, wpool_ref, pscale_ref, convw_ref, convb_ref,
                  wgate_ref, brg_ref, big_ref, lam_ref, proj_ref, yp_ref, a_ref, b_ref,
                  pool_tail, lru_tail):
    TM = x_ref.shape[0]
    t = pl.program_id(1)

    @pl.when(t == 0)
    def _():
        pool_tail[...] = jnp.zeros_like(pool_tail)
        lru_tail[...] = jnp.zeros_like(lru_tail)

    h = _rms(x_ref[...], g_ref[...]).astype(BF16)

    def project(c):
        return jnp.dot(h, w_ref[:, c * D_MODEL:(c + 1) * D_MODEL], preferred_element_type=F32)

    def head_norm(r, c):
        heads = []
        for hd in range(SB_HEADS):
            seg = r[:, hd * SB_HEAD_DIM:(hd + 1) * SB_HEAD_DIM]
            ms = jnp.mean(seg * seg, axis=-1, keepdims=True)
            heads.append(seg * lax.rsqrt(ms + EPS))
        return jnp.concatenate(heads, axis=1) * qkg_ref[c - Q_CHUNK:c - Q_CHUNK + 1, :]

    def emit(c):
        r = project(c)
        if c in (Q_CHUNK, K_CHUNK):
            r = head_norm(r, c)
        proj_ref[:, (c - Q_CHUNK) * D_MODEL:(c - Q_CHUNK + 1) * D_MODEL] = r.astype(BF16)

    u_pool = project(POOL_CHUNK)
    u_lru = project(LRU_CHUNK)

    pos = t * TM + 1 + lax.broadcasted_iota(jnp.int32, (POOL_HALO, CH_TILE), 0)
    lam = lam_ref[...]
    softplus_neg_lam = jnp.maximum(-lam, 0.0) + jnp.log1p(jnp.exp(-jnp.abs(lam)))

    later_chunks = list(range(Q_CHUNK, IN_COLS // D_MODEL))
    for j in range(D_MODEL // CH_TILE):
        cols = slice(j * CH_TILE, (j + 1) * CH_TILE)

        window = 2 << j
        u = u_pool[:, cols]
        s = jnp.concatenate([pool_tail[:, cols], u], axis=0)
        pool_tail[:, cols] = u[TM - POOL_HALO:]
        for d in (1, 2, 4, 8)[:j + 1]:
            s = s + pltpu.roll(s, d, axis=0)
        s = s[POOL_HALO:]
        head_inv = 1.0 / jnp.minimum(pos, window).astype(F32)
        mean = jnp.concatenate([s[:POOL_HALO] * head_inv, s[POOL_HALO:] * (1.0 / window)], axis=0)
        y = jnp.dot((mean - u).astype(BF16), wpool_ref[j], preferred_element_type=F32)
        yp_ref[:, cols] = (y * pscale_ref[:, cols]).astype(BF16)

        ul = u_lru[:, cols]
        hist = jnp.concatenate([lru_tail[:, cols], ul], axis=0)
        lru_tail[:, cols] = ul[TM - ROW_TILE:]
        xc = convb_ref[:, cols] + convw_ref[CONV_WIDTH - 1:CONV_WIDTH, cols] * ul
        for k in range(CONV_WIDTH - 1):
            shifted = pltpu.roll(hist, CONV_WIDTH - 1 - k, axis=0)[ROW_TILE:]
            xc = xc + convw_ref[k:k + 1, cols] * shifted
        gates = jnp.dot(xc.astype(BF16), wgate_ref[j], preferred_element_type=F32)
        decay = (-0.5 * LRU_C) * softplus_neg_lam[:, cols]
        log_a = decay + decay * jnp.tanh(0.5 * (gates[:, :CH_TILE] + brg_ref[:, cols]))
        ig = 0.5 + 0.5 * jnp.tanh(0.5 * (gates[:, CH_TILE:] + big_ref[:, cols]))
        a = jnp.exp(log_a)
        a_ref[:, cols] = a
        b_ref[:, cols] = jnp.sqrt(-jnp.tanh(log_a) * (a * a + 1.0)) * (ig * xc)

        n_emit = -(-len(later_chunks) // (D_MODEL // CH_TILE - j))
        for c in later_chunks[:n_emit]:
            emit(c)
        later_chunks = later_chunks[n_emit:]


def _front(x2, B, S, layer, g, w, qk_gain, wpool, pscale, convw, convb, wgate, brg, big, lam, tm):
    T = B * S
    nt = S // tm
    tile = lambda width: pl.BlockSpec((tm, width), lambda b, t: (b * nt + t, 0))
    vec = _const_spec((1, D_MODEL))
    return pl.pallas_call(
        _front_kernel,
        grid=(B, nt),
        in_specs=[tile(D_MODEL), vec,
                  _const_spec((D_MODEL, IN_COLS), layer),
                  _const_spec((2, D_MODEL)),
                  _const_spec((N_POOL_GROUPS, POOL_GROUP, POOL_GROUP), layer),
                  vec,
                  _const_spec((CONV_WIDTH, D_MODEL)),
                  vec,
                  _const_spec((D_MODEL // CH_TILE, CH_TILE, 2 * CH_TILE), layer),
                  vec, vec, vec],
        out_specs=[tile(PROJ_COLS), tile(D_MODEL), tile(D_MODEL), tile(D_MODEL)],
        out_shape=[jax.ShapeDtypeStruct((T, PROJ_COLS), BF16),
                   jax.ShapeDtypeStruct((T, D_MODEL), BF16),
                   jax.ShapeDtypeStruct((T, D_MODEL), F32),
                   jax.ShapeDtypeStruct((T, D_MODEL), F32)],
        scratch_shapes=[pltpu.VMEM((POOL_HALO, D_MODEL), F32), pltpu.VMEM((ROW_TILE, D_MODEL), F32)],
        compiler_params=_cparams("parallel", "arbitrary"),
        name="front",
    )(x2, g, w, qk_gain, wpool, pscale, convw, convb, wgate, brg, big, lam)


def _attn_kernel(q_ref, k_ref, v_ref, o_ref, *scratch):
    S = q_ref.shape[0]
    carry_scr, acc_scr = scratch[:HEADS_PER_STEP], scratch[HEADS_PER_STEP:]

    rr = lax.broadcasted_iota(jnp.int32, (2 * KEY_BLOCK, 2 * KEY_BLOCK), 0) % KEY_BLOCK
    cc = lax.broadcasted_iota(jnp.int32, (2 * KEY_BLOCK, 2 * KEY_BLOCK), 1)
    suffix = jnp.where((cc >= KEY_BLOCK) | (rr > cc), 1.0, 0.0).astype(BF16)
    heads = range(HEADS_PER_STEP)
    hcols = [slice(hd * SB_HEAD_DIM, (hd + 1) * SB_HEAD_DIM) for hd in heads]

    def fold(jobs):
        def visible(job):
            _, _, n_rows, _, n_keys, lead = job
            if lead is None:
                return None
            r = lax.broadcasted_iota(jnp.int32, (n_rows, n_keys), 0)
            c = lax.broadcasted_iota(jnp.int32, (n_rows, n_keys), 1)
            return c < r + lead

        work = [(job, visible(job), hd) for job in jobs for hd in heads]
        z = [lax.dot_general(q_ref[pl.ds(row0 + sub0, n_rows), hcols[hd]],
                             k_ref[pl.ds(pl.multiple_of(key0, Q_TILE), n_keys), hcols[hd]],
                             (((1,), (1,)), ((), ())), preferred_element_type=F32)
             for (row0, sub0, n_rows, key0, n_keys, _), _, hd in work]
        log_beta, sums = [], []
        for (job, mask, hd), zz in zip(work, z):
            lo_z = jnp.minimum(zz, 0.0)
            hi_z = jnp.maximum(zz, 0.0)
            softplus = jnp.log(1.0 + jnp.exp2(lo_z - hi_z)) * LOG2E
            log_beta.append(lo_z - softplus)
            drop = hi_z + softplus
            if mask is not None:
                drop = jnp.where(mask, drop, 0.0)
            hi = drop.astype(BF16)
            lo = (drop - hi.astype(F32)).astype(BF16)
            sums.append([jnp.dot(jnp.concatenate([hi[:, c * KEY_BLOCK:(c + 1) * KEY_BLOCK],
                                                  lo[:, c * KEY_BLOCK:(c + 1) * KEY_BLOCK]], axis=1),
                                 suffix, preferred_element_type=F32)
                         for c in range(job[4] // KEY_BLOCK)])
        for n, (job, mask, hd) in enumerate(work):
            _, sub0, n_rows, key0, n_keys, _ = job
            rows = slice(sub0, sub0 + n_rows)
            n_blk = n_keys // KEY_BLOCK
            carry = carry_scr[hd][rows, :]
            ws = [None] * n_blk
            for c in reversed(range(n_blk)):
                cols = slice(c * KEY_BLOCK, (c + 1) * KEY_BLOCK)
                w = jnp.exp2(log_beta[n][:, cols] - sums[n][c][:, :KEY_BLOCK] - carry)
                if mask is not None:
                    w = jnp.where(mask[:, cols], w, 0.0)
                ws[c] = w.astype(BF16)
                carry = carry + sums[n][c][:, KEY_BLOCK:]
            carry_scr[hd][rows, :] = carry
            acc_scr[hd][rows, :] += jnp.dot(
                jnp.concatenate(ws, axis=1),
                v_ref[pl.ds(pl.multiple_of(key0, Q_TILE), n_keys), hcols[hd]],
                preferred_element_type=F32)

    def q_tile(i, first):
        row0 = pl.multiple_of(i * Q_TILE, Q_TILE)
        for ref in scratch:
            ref[...] = jnp.zeros_like(ref)

        def chunk(n):
            return (row0, 0, Q_TILE, row0 - n * Q_TILE, Q_TILE, None)

        half = Q_TILE // 2
        diagonal = [(row0, 0, half, row0, half, 0), (row0, half, half, row0, Q_TILE, half)]
        if first:
            fold(diagonal)
        else:
            fold(diagonal + [chunk(1)])

            def weights_alive():
                least = functools.reduce(jnp.minimum, [ref[...] for ref in carry_scr])
                return jnp.min(least) < UNDERFLOW_BITS

            def earlier(state):
                n, _ = state
                fold([chunk(n)])
                return n + 1, weights_alive()

            lax.while_loop(lambda state: (state[0] <= i) & state[1], earlier,
                           (jnp.int32(2), weights_alive()))
        for hd in heads:
            o_ref[pl.ds(row0, Q_TILE), hcols[hd]] = acc_scr[hd][...].astype(BF16)

    q_tile(0, True)

    def later_tile(i, _):
        q_tile(i, False)
        return 0

    lax.fori_loop(1, S // Q_TILE, later_tile, 0)


def _attn(proj, B, S):
    T = B * S
    blk = lambda col0: pl.BlockSpec((S, HEAD_COLS), lambda b, h: (b, col0 + h))
    return pl.pallas_call(
        _attn_kernel,
        grid=(B, SB_HEADS // HEADS_PER_STEP),
        in_specs=[blk(Q_COL0), blk(K_COL0), blk(V_COL0)],
        out_specs=pl.BlockSpec((S, HEAD_COLS), lambda b, h: (b, h)),
        out_shape=jax.ShapeDtypeStruct((T, D_MODEL), BF16),
        scratch_shapes=([pltpu.VMEM((Q_TILE, KEY_BLOCK), F32)] * HEADS_PER_STEP
                        + [pltpu.VMEM((Q_TILE, SB_HEAD_DIM), F32)] * HEADS_PER_STEP),
        compiler_params=_cparams("parallel", "parallel"),
        name="attn",
    )(proj, proj, proj)


def _merge_kernel(x_ref, yp_ref, a_ref, b_ref, ys_ref, g0_ref, g1_ref, g2_ref, bg_ref, wb_ref,
                  wo_ref, o_ref, h_state, a_scr, b_scr, yl_scr):
    TM = x_ref.shape[0]

    @pl.when(pl.program_id(1) == 0)
    def _():
        h_state[...] = jnp.zeros_like(h_state)

    sub = lax.broadcasted_iota(jnp.int32, (TM // ROW_TILE, ROW_TILE, CH_TILE), 1)
    for j in range(D_MODEL // CH_TILE):
        cols = slice(j * CH_TILE, (j + 1) * CH_TILE)
        a = a_ref[:, cols].reshape(TM // ROW_TILE, ROW_TILE, CH_TILE)
        b = b_ref[:, cols].reshape(TM // ROW_TILE, ROW_TILE, CH_TILE)
        for d in (1, 2, 4):
            keep = sub >= d
            b = a * jnp.where(keep, pltpu.roll(b, d, axis=1), 0.0) + b
            a = a * jnp.where(keep, pltpu.roll(a, d, axis=1), 1.0)
        a_scr[:, :, cols] = a
        b_scr[:, :, cols] = b

    def scan_tile(n, h_prev):
        hh = a_scr[n] * h_prev + b_scr[n]
        yl_scr[pl.ds(pl.multiple_of(n * ROW_TILE, ROW_TILE), ROW_TILE), :] = hh
        return jnp.broadcast_to(hh[ROW_TILE - 1:ROW_TILE, :], (ROW_TILE, D_MODEL))

    h_state[...] = lax.fori_loop(0, TM // ROW_TILE, scan_tile, h_state[...], unroll=4)

    branches = ((yp_ref[...], g0_ref), (yl_scr[...].astype(BF16), g1_ref), (ys_ref[...], g2_ref))
    merged = None
    for n, (y, g_ref) in enumerate(branches):
        gate = jax.nn.sigmoid(g_ref[...].astype(F32) + bg_ref[n:n + 1, :])
        term = gate * jnp.dot(y, wb_ref[n], preferred_element_type=F32)
        merged = term if merged is None else merged + term
    o_ref[...] = x_ref[...] + jnp.dot(merged.astype(BF16), wo_ref[...],
                                      preferred_element_type=F32)


def _merge(x2, yp, a, b, ys, proj, B, S, layer, bg, wb, wo, tm):
    T = B * S
    nt = S // tm
    tile = pl.BlockSpec((tm, D_MODEL), lambda i, t: (i * nt + t, 0))
    gate = lambda n: pl.BlockSpec((tm, D_MODEL), lambda i, t: (i * nt + t, GATE_COL0 + n))
    return pl.pallas_call(
        _merge_kernel,
        grid=(B, nt),
        in_specs=[tile, tile, tile, tile, tile, gate(0), gate(1), gate(2),
                  _const_spec((3, D_MODEL)),
                  _const_spec((3, D_MODEL, D_MODEL), layer),
                  _const_spec((D_MODEL, D_MODEL), layer)],
        out_specs=tile,
        out_shape=jax.ShapeDtypeStruct((T, D_MODEL), F32),
        scratch_shapes=[pltpu.VMEM((ROW_TILE, D_MODEL), F32),
                        pltpu.VMEM((tm // ROW_TILE, ROW_TILE, D_MODEL), F32),
                        pltpu.VMEM((tm // ROW_TILE, ROW_TILE, D_MODEL), F32),
                        pltpu.VMEM((tm, D_MODEL), F32)],
        compiler_params=_cparams("parallel", "arbitrary"),
        name="merge",
    )(x2, yp, a, b, ys, proj, proj, proj, bg, wb, wo)


def _ffn_kernel(x_ref, g_ref, wg_ref, wu_ref, wd_ref, o_ref):
    x = x_ref[...]
    h = _rms(x, g_ref[...]).astype(BF16)
    gate = jnp.dot(h, wg_ref[...], preferred_element_type=F32)
    up = jnp.dot(h, wu_ref[...], preferred_element_type=F32)
    act = (gate * jax.nn.sigmoid(gate) * up).astype(BF16)
    o_ref[...] = x + jnp.dot(act, wd_ref[...], preferred_element_type=F32)


def _ffn(x2, layer, g, wg, wu, wd, tm):
    T = x2.shape[0]
    tile = pl.BlockSpec((tm, D_MODEL), lambda i: (i, 0))
    return pl.pallas_call(
        _ffn_kernel,
        grid=(T // tm,),
        in_specs=[tile, _const_spec((1, D_MODEL)), _const_spec((D_MODEL, D_FF), layer),
                  _const_spec((D_MODEL, D_FF), layer), _const_spec((D_FF, D_MODEL), layer)],
        out_specs=tile,
        out_shape=jax.ShapeDtypeStruct((T, D_MODEL), F32),
        compiler_params=_cparams("parallel"),
        name="ffn",
    )(x2, g, wg, wu, wd)


def _gate_tiles(w_rg, w_ig):
    depth = w_rg.shape[0]
    nt = D_MODEL // CH_TILE
    per = CH_TILE // LRU_BLOCK
    eye = jnp.eye(per, dtype=F32)

    def tiles(w):
        w5 = w.reshape(depth, nt, per, LRU_BLOCK, LRU_BLOCK)
        return jnp.einsum('ltbij,bc->ltbicj', w5, eye).reshape(depth, nt, CH_TILE, CH_TILE)

    return jnp.concatenate([tiles(w_rg), tiles(w_ig)], axis=-1).astype(BF16)


def kernel(x, norm_mix, w_in, b_gate, w_pool, pool_scale, conv_w, conv_b, w_rg, b_rg, w_ig, b_ig,
           lru_lambda, q_norm, k_norm, w_branch, w_out, norm_ffn, w_ffn_gate, w_ffn_up,
           w_ffn_down):
    B, S, D = x.shape
    assert D == D_MODEL and S % Q_TILE == 0
    T = B * S
    tm = min(512, S)
    assert S % tm == 0
    depth = w_in.shape[0]
    row = lambda v: v.reshape(1, -1)

    w_in, w_pool, w_branch, w_out, w_ffn_gate, w_ffn_up, w_ffn_down = (
        w.astype(BF16) for w in (w_in, w_pool, w_branch, w_out, w_ffn_gate, w_ffn_up, w_ffn_down))
    w_gate = _gate_tiles(w_rg, w_ig)

    x2 = x.reshape(T, D)
    for l in range(depth):
        q_gain = q_norm[l] * (SB_HEAD_DIM ** -0.5 * LOG2E)
        qk_gain = jnp.stack([jnp.tile(q_gain, SB_HEADS), jnp.tile(k_norm[l], SB_HEADS)])
        proj, yp, lru_a, lru_b = _front(x2, B, S, l, row(norm_mix[l]), w_in, qk_gain, w_pool,
                              row(pool_scale[l]), conv_w[l], row(conv_b[l]), w_gate,
                              row(b_rg[l]), row(b_ig[l]), row(lru_lambda[l]), tm)
        ys = _attn(proj, B, S)
        x2 = _merge(x2, yp, lru_a, lru_b, ys, proj, B, S, l, b_gate[l].reshape(3, D), w_branch,
                    w_out, tm)
        x2 = _ffn(x2, l, row(norm_ffn[l]), w_ffn_gate, w_ffn_up, w_ffn_down, tm)
    return x2.reshape(B, S, D)
```
